```python
import jax, jax.numpy as jnp
from jax import lax
import numpy as np

D_MODEL = 1024
BATCH = 4
SEQ = 8192
DEPTH = 1

RET_HEADS = 4
RET_QK_DIM = 128
RET_V_DIM = 256
RET_CHUNK = 128
ATTN_Q_HEADS = 16
ATTN_KV_HEADS = 2
ATTN_HEAD_DIM = 64
WINDOW = 128
ATTN_BLOCK = 128
D_FF = -(-8 * D_MODEL // (3 * 256)) * 256
ROPE_THETA = 10000.0
EPS = 1e-6

RET_QK = RET_HEADS * RET_QK_DIM
RET_V = RET_HEADS * RET_V_DIM
ATTN_Q = ATTN_Q_HEADS * ATTN_HEAD_DIM
ATTN_KV = ATTN_KV_HEADS * ATTN_HEAD_DIM
SPLITS = [RET_QK, RET_QK, RET_V, RET_V, ATTN_Q, ATTN_KV, ATTN_KV, D_MODEL, D_MODEL]
D_IN = sum(SPLITS)
SPLIT_IDX = [int(v) for v in np.cumsum(SPLITS)[:-1]]

kernel_name = "hybrid_retention_swa_sink_gated_block"


def rms_norm(x, g):
    xf = x.astype(jnp.float32)
    y = xf * lax.rsqrt(jnp.mean(xf * xf, axis=-1, keepdims=True) + EPS)
    return (y * g.astype(jnp.float32)).astype(x.dtype)


def rotary(x, pos):
    d = x.shape[-1]
    half = d // 2
    inv_freq = ROPE_THETA ** (-jnp.arange(half, dtype=jnp.float32) / half)
    ang = pos.astype(jnp.float32)[:, None] * inv_freq[None, :]
    cos = jnp.cos(ang)[None, :, None, :]
    sin = jnp.sin(ang)[None, :, None, :]
    xf = x.astype(jnp.float32)
    x1, x2 = xf[..., :half], xf[..., half:]
    out = jnp.concatenate([x1 * cos - x2 * sin, x2 * cos + x1 * sin], axis=-1)
    return out.astype(x.dtype)


def retention_chunkwise(q, k, v):
    B, S, H, dk = q.shape
    dv = v.shape[-1]
    C = RET_CHUNK
    N = S // C
    log_gamma = jnp.log1p(-jnp.exp2(-5.0 - jnp.arange(H, dtype=jnp.float32)))
    idx = jnp.arange(C, dtype=jnp.float32)
    rel = idx[:, None] - idx[None, :]
    intra_decay = jnp.where(rel[None] >= 0,
                            jnp.exp(log_gamma[:, None, None] * jnp.maximum(rel, 0.0)[None]), 0.0)
    q_decay = jnp.exp(log_gamma[:, None] * (idx + 1.0))[None, :, :, None]
    k_decay = jnp.exp(log_gamma[:, None] * (C - 1.0 - idx))[None, :, :, None]
    chunk_decay = jnp.exp(log_gamma * C)[None, :, None, None]

    qf = q.astype(jnp.float32) * (dk ** -0.5)
    kf = k.astype(jnp.float32)
    vf = v.astype(jnp.float32)
    to_chunks = lambda t: t.reshape(B, N, C, H, t.shape[-1]).transpose(1, 0, 3, 2, 4)
    qc, kc, vc = to_chunks(qf), to_chunks(kf), to_chunks(vf)

    def step(state, inp):
        qn, kn, vn = inp
        scores = jnp.einsum('bhcd,bhsd->bhcs', qn, kn) * intra_decay
        inner = jnp.einsum('bhcs,bhse->bhce', scores, vn)
        cross = jnp.einsum('bhcd,bhde->bhce', qn, state) * q_decay
        new_state = state * chunk_decay + jnp.einsum('bhsd,bhse->bhde', kn * k_decay, vn)
        return new_state, inner + cross

    state0 = jnp.zeros((B, H, dk, dv), jnp.float32)
    _, out = lax.scan(step, state0, (qc, kc, vc))
    return out.transpose(1, 0, 3, 2, 4).reshape(B, S, H, dv)


def head_group_norm(y, g):
    B, S, H, dv = y.shape
    mu = jnp.mean(y, axis=-1, keepdims=True)
    yc = y - mu
    var = jnp.mean(yc * yc, axis=-1, keepdims=True)
    yn = (yc * lax.rsqrt(var + EPS)).reshape(B, S, H * dv)
    return yn * g.astype(jnp.float32)


def sliding_window_sink_attention(q, k, v, sinks):
    B, S, Hq, d = q.shape
    Hkv = k.shape[2]
    G = Hq // Hkv
    C = ATTN_BLOCK
    N = S // C
    qb = q.reshape(B, N, C, Hkv, G, d)
    pad = ((0, 0), (C, 0), (0, 0), (0, 0))
    kp = jnp.pad(k, pad).reshape(B, N + 1, C, Hkv, d)
    vp = jnp.pad(v, pad).reshape(B, N + 1, C, Hkv, d)
    kb = jnp.concatenate([kp[:, :-1], kp[:, 1:]], axis=2)
    vb = jnp.concatenate([vp[:, :-1], vp[:, 1:]], axis=2)
    scores = jnp.einsum('bnqhgd,bnkhd->bnhgqk', qb, kb).astype(jnp.float32) * (d ** -0.5)
    qi = jnp.arange(C)[:, None]
    kj = jnp.arange(2 * C)[None, :]
    rel = C + qi - kj
    key_pos = jnp.arange(N)[:, None, None] * C + kj[None] - C
    mask = (rel[None] >= 0) & (rel[None] < WINDOW) & (key_pos >= 0)
    scores = jnp.where(mask[None, :, None, None], scores, -1e30)
    sink = sinks.astype(jnp.float32).reshape(1, 1, Hkv, G, 1, 1)
    m = jnp.maximum(jnp.max(scores, axis=-1, keepdims=True), sink)
    e = jnp.exp(scores - m)
    probs = e / (jnp.sum(e, axis=-1, keepdims=True) + jnp.exp(sink - m))
    out = jnp.einsum('bnhgqk,bnkhd->bnqhgd', probs.astype(v.dtype), vb)
    return out.reshape(B, S, Hq, d)


def setup_inputs(seed: int = 0) -> dict:
    key = jax.random.key(seed)
    ks = jax.random.split(key, 16)
    nrm = lambda k, shape, fan_in: jax.random.normal(k, shape, jnp.float32) * (fan_in ** -0.5)
    gain = lambda k, shape: 1.0 + 0.02 * jax.random.normal(k, shape, jnp.float32)
    return {
        "x": jax.random.normal(ks[0], (BATCH, SEQ, D_MODEL), jnp.float32),
        "ln1_g": gain(ks[1], (DEPTH, D_MODEL)),
        "w_in": nrm(ks[2], (DEPTH, D_MODEL, D_IN), D_MODEL),
        "b_in": 0.02 * jax.random.normal(ks[3], (DEPTH, D_IN), jnp.float32),
        "ret_norm_g": gain(ks[4], (DEPTH, RET_V)),
        "w_ret_out": nrm(ks[5], (DEPTH, RET_V, D_MODEL), RET_V),
        "attn_sinks": 0.5 * jax.random.normal(ks[6], (DEPTH, ATTN_Q_HEADS), jnp.float32),
        "w_attn_out": nrm(ks[7], (DEPTH, ATTN_Q, D_MODEL), ATTN_Q),
        "w_out": nrm(ks[8], (DEPTH, D_MODEL, D_MODEL), D_MODEL),
        "ln2_g": gain(ks[9], (DEPTH, D_MODEL)),
        "w_ffn_gate": nrm(ks[10], (DEPTH, D_MODEL, D_FF), D_MODEL),
        "w_ffn_up": nrm(ks[11], (DEPTH, D_MODEL, D_FF), D_MODEL),
        "w_ffn_down": nrm(ks[12], (DEPTH, D_FF, D_MODEL), D_FF),
        "lnf_g": gain(ks[13], (D_MODEL,)),
    }


def reference(x, ln1_g, w_in, b_in, ret_norm_g, w_ret_out, attn_sinks, w_attn_out, w_out,
              ln2_g, w_ffn_gate, w_ffn_up, w_ffn_down, lnf_g):
    B, S, _ = x.shape
    pos = jnp.arange(S, dtype=jnp.int32)
    for l in range(DEPTH):
        h = rms_norm(x, ln1_g[l])
        proj = h @ w_in[l] + b_in[l]
        rq, rk, rv, rg, aq, ak, av, gate_a, gate_b = jnp.split(proj, SPLIT_IDX, axis=-1)

        rq = rotary(rq.reshape(B, S, RET_HEADS, RET_QK_DIM), pos)
        rk = rotary(rk.reshape(B, S, RET_HEADS, RET_QK_DIM), pos)
        ry = retention_chunkwise(rq, rk, rv.reshape(B, S, RET_HEADS, RET_V_DIM))
        ry = head_group_norm(ry, ret_norm_g[l]).astype(x.dtype)
        branch_a = (jax.nn.silu(rg) * ry) @ w_ret_out[l]

        aq = rotary(aq.reshape(B, S, ATTN_Q_HEADS, ATTN_HEAD_DIM), pos)
        ak = rotary(ak.reshape(B, S, ATTN_KV_HEADS, ATTN_HEAD_DIM), pos)
        ay = sliding_window_sink_attention(aq, ak, av.reshape(B, S, ATTN_KV_HEADS, ATTN_HEAD_DIM),
                                           attn_sinks[l])
        branch_b = ay.reshape(B, S, ATTN_Q) @ w_attn_out[l]

        merged = jax.nn.sigmoid(gate_a) * branch_a + jax.nn.sigmoid(gate_b) * branch_b
        x = x + merged @ w_out[l]

        h2 = rms_norm(x, ln2_g[l])
        x = x + (jax.nn.silu(h2 @ w_ffn_gate[l]) * (h2 @ w_ffn_up[l])) @ w_ffn_down[l]
    return rms_norm(x, lnf_g)
```

```python
import functools

import jax
import jax.numpy as jnp
from jax import lax
from jax.experimental import pallas as pl
from jax.experimental.pallas import tpu as pltpu

F32 = jnp.float32
BF16 = jnp.bfloat16

D_MODEL = 1024
RET_HEADS = 4
RET_QK_DIM = 128
RET_V_DIM = 256
RET_CHUNK = 128
ATTN_Q_HEADS = 16
ATTN_KV_HEADS = 2
ATTN_HEAD_DIM = 64
WINDOW = 128
ATTN_BLOCK = 128
D_FF = 2816
ROPE_THETA = 10000.0
EPS = 1e-6
MASK_VALUE = -1e30

RET_QK = RET_HEADS * RET_QK_DIM
RET_V = RET_HEADS * RET_V_DIM
ATTN_Q = ATTN_Q_HEADS * ATTN_HEAD_DIM
ATTN_KV = ATTN_KV_HEADS * ATTN_HEAD_DIM
OFF_RQ = 0
OFF_RK = OFF_RQ + RET_QK
OFF_RV = OFF_RK + RET_QK
OFF_RG = OFF_RV + RET_V
OFF_AQ = OFF_RG + RET_V
OFF_AK = OFF_AQ + ATTN_Q
OFF_AV = OFF_AK + ATTN_KV
OFF_GA = OFF_AV + ATTN_KV
OFF_GB = OFF_GA + D_MODEL
D_IN = OFF_GB + D_MODEL

LANES = 128
TOKEN_TILE = 512
VMEM_LIMIT_BYTES = 56 * 1024 * 1024


def _sigmoid(x):
    return 1.0 / (1.0 + jnp.exp(-x))


def _rms_norm(x, g):
    ms = jnp.mean(x * x, axis=-1, keepdims=True)
    return (x * lax.rsqrt(ms + EPS)) * g


def _resident(shape):
    nd = len(shape)
    return pl.BlockSpec(shape, lambda *_: (0,) * nd, pipeline_mode=pl.Buffered(1))


def _rot_half128(x, cos, sin_signed):
    return x * cos + pltpu.roll(x, 64, 1) * sin_signed


def _rot_half64(x, cos, sin_signed, first_half):
    partner = jnp.where(first_half, pltpu.roll(x, 96, 1), pltpu.roll(x, 32, 1))
    return x * cos + partner * sin_signed


def _inproj_kernel(x_ref, g_ref, w_ref, b_ref, cr_ref, sr_ref, ca_ref, sa_ref,
                   rq_ref, rk_ref, rv_ref, rg_ref, aq_ref, ak_ref, av_ref, ga_ref, gb_ref):
    h = _rms_norm(x_ref[...], g_ref[...]).astype(BF16)

    def proj(off, width):
        acc = jnp.dot(h, w_ref[:, off:off + width], preferred_element_type=F32)
        return acc + b_ref[:, off:off + width]

    cr, sr = cr_ref[...], sr_ref[...]
    ca, sa = ca_ref[...], sa_ref[...]
    lane = lax.broadcasted_iota(jnp.int32, ca.shape, 1)
    first_half = (lane % ATTN_HEAD_DIM) < (ATTN_HEAD_DIM // 2)

    q_scale = RET_QK_DIM ** -0.5
    for hd in range(RET_HEADS):
        sl = slice(hd * LANES, (hd + 1) * LANES)
        rq = proj(OFF_RQ + hd * LANES, LANES)
        rq_ref[:, sl] = (_rot_half128(rq, cr, sr) * q_scale).astype(BF16)
        rk = proj(OFF_RK + hd * LANES, LANES)
        rk_ref[:, sl] = _rot_half128(rk, cr, sr).astype(BF16)
    rv_ref[...] = proj(OFF_RV, RET_V).astype(BF16)
    rg_ref[...] = proj(OFF_RG, RET_V).astype(BF16)
    a_scale = ATTN_HEAD_DIM ** -0.5
    for c in range(ATTN_Q // LANES):
        sl = slice(c * LANES, (c + 1) * LANES)
        aq = proj(OFF_AQ + c * LANES, LANES)
        aq_ref[:, sl] = (_rot_half64(aq, ca, sa, first_half) * a_scale).astype(BF16)
    ak = proj(OFF_AK, ATTN_KV)
    ak_ref[...] = _rot_half64(ak, ca, sa, first_half).astype(BF16)
    av_ref[...] = proj(OFF_AV, ATTN_KV).astype(BF16)
    ga_ref[...] = proj(OFF_GA, D_MODEL).astype(BF16)
    gb_ref[...] = proj(OFF_GB, D_MODEL).astype(BF16)


def _inproj(x2d, ln_g, w_in, b_in, tables, seq_len):
    t = x2d.shape[0]
    tm = TOKEN_TILE
    pos_tiles = seq_len // tm
    row = lambda i: (i, 0)
    pos = lambda i: (i % pos_tiles, 0)
    widths = (RET_QK, RET_QK, RET_V, RET_V, ATTN_Q, ATTN_KV, ATTN_KV, D_MODEL, D_MODEL)
    return pl.pallas_call(
        _inproj_kernel,
        grid=(t // tm,),
        in_specs=[
            pl.BlockSpec((tm, D_MODEL), row),
            _resident((1, D_MODEL)),
            _resident((D_MODEL, D_IN)),
            _resident((1, D_IN)),
            pl.BlockSpec((tm, LANES), pos),
            pl.BlockSpec((tm, LANES), pos),
            pl.BlockSpec((tm, LANES), pos),
            pl.BlockSpec((tm, LANES), pos),
        ],
        out_specs=[pl.BlockSpec((tm, w), row) for w in widths],
        out_shape=[jax.ShapeDtypeStruct((t, w), BF16) for w in widths],
        compiler_params=pltpu.CompilerParams(
            dimension_semantics=("parallel",), vmem_limit_bytes=VMEM_LIMIT_BYTES),
        name="inproj",
    )(x2d, ln_g, w_in, b_in, *tables)


def _ret_kernel(q_ref, k_ref, v_ref, rg_ref, intra_ref, qdec_ref, kdec_ref, cdec_ref, gn_ref,
                o_ref, state_ref):
    @pl.when(pl.program_id(1) == 0)
    def _():
        state_ref[...] = jnp.zeros_like(state_ref)

    for hd in range(RET_HEADS):
        qk = slice(hd * RET_QK_DIM, (hd + 1) * RET_QK_DIM)
        vs = slice(hd * RET_V_DIM, (hd + 1) * RET_V_DIM)
        q = q_ref[:, qk]
        k = k_ref[:, qk]
        v = v_ref[:, vs]
        scores = lax.dot_general(q, k, (((1,), (1,)), ((), ())), preferred_element_type=F32)
        scores = scores * intra_ref[hd]
        inner = jnp.dot(scores.astype(BF16), v, preferred_element_type=F32)
        state = state_ref[hd]
        cross = jnp.dot(q, state.astype(BF16), preferred_element_type=F32) * qdec_ref[hd]
        kd = (k.astype(F32) * kdec_ref[hd]).astype(BF16)
        update = lax.dot_general(kd, v, (((0,), (0,)), ((), ())), preferred_element_type=F32)
        state_ref[hd] = state * cdec_ref[hd] + update

        y = inner + cross
        yc = y - jnp.mean(y, axis=-1, keepdims=True)
        var = jnp.mean(yc * yc, axis=-1, keepdims=True)
        yn = (yc * lax.rsqrt(var + EPS)) * gn_ref[:, vs]
        gate = rg_ref[:, vs].astype(F32)
        o_ref[:, vs] = ((gate * _sigmoid(gate)) * yn).astype(BF16)


def _retention(rq, rk, rv, rg, decays, gn_g, batch, seq_len):
    t = rq.shape[0]
    c = RET_CHUNK
    n = seq_len // c
    row = lambda b, i: (b * n + i, 0)
    return pl.pallas_call(
        _ret_kernel,
        grid=(batch, n),
        in_specs=[
            pl.BlockSpec((c, RET_QK), row),
            pl.BlockSpec((c, RET_QK), row),
            pl.BlockSpec((c, RET_V), row),
            pl.BlockSpec((c, RET_V), row),
            _resident((RET_HEADS, c, c)),
            _resident((RET_HEADS, c, RET_V_DIM)),
            _resident((RET_HEADS, c, RET_QK_DIM)),
            _resident((RET_HEADS, 1, RET_V_DIM)),
            _resident((1, RET_V)),
        ],
        out_specs=pl.BlockSpec((c, RET_V), row),
        out_shape=jax.ShapeDtypeStruct((t, RET_V), BF16),
        scratch_shapes=[pltpu.VMEM((RET_HEADS, RET_QK_DIM, RET_V_DIM), F32)],
        compiler_params=pltpu.CompilerParams(
            dimension_semantics=("parallel", "arbitrary"), vmem_limit_bytes=VMEM_LIMIT_BYTES),
        name="retention",
    )(rq, rk, rv, rg, *decays, gn_g)


def _attn_kernel(sink_ref, q_ref, kp_ref, kc_ref, vp_ref, vc_ref, cap_ref, o_ref):
    c = ATTN_BLOCK
    first = pl.program_id(1) == 0
    cap = cap_ref[jnp.where(first, 0, 1)]
    low_kv = lax.broadcasted_iota(jnp.int32, (2 * c, LANES), 1) < ATTN_HEAD_DIM
    low_q = lax.broadcasted_iota(jnp.int32, (c, LANES), 1) < ATTN_HEAD_DIM
    top_rows = lax.broadcasted_iota(jnp.int32, (2 * c, 1), 0) < c

    keys = jnp.concatenate([kp_ref[...], kc_ref[...]], axis=0).astype(F32)
    vals = jnp.concatenate([vp_ref[...], vc_ref[...]], axis=0).astype(F32)
    keys_sw = pltpu.roll(keys, ATTN_HEAD_DIM, 1)
    vals_sw = pltpu.roll(vals, ATTN_HEAD_DIM, 1)
    pairs_per_kv = ATTN_Q_HEADS // ATTN_KV_HEADS // 2
    for g in range(ATTN_KV_HEADS):
        if g == 0:
            kk = jnp.where(low_kv, keys, keys_sw).astype(BF16)
            vv = jnp.where(low_kv, vals, vals_sw).astype(BF16)
        else:
            kk = jnp.where(low_kv, keys_sw, keys).astype(BF16)
            vv = jnp.where(low_kv, vals_sw, vals).astype(BF16)
        for j in range(pairs_per_kv):
            p = g * pairs_per_kv + j
            sl = slice(p * LANES, (p + 1) * LANES)
            qp = q_ref[:, sl]
            zero = jnp.zeros_like(qp)
            q2 = jnp.concatenate([jnp.where(low_q, qp, zero), jnp.where(low_q, zero, qp)], axis=0)
            s = lax.dot_general(q2, kk, (((1,), (1,)), ((), ())), preferred_element_type=F32)
            s = jnp.minimum(s, cap)
            sink = jnp.where(top_rows, sink_ref[2 * p], sink_ref[2 * p + 1])
            m = jnp.maximum(jnp.max(s, axis=-1, keepdims=True), sink)
            e = jnp.exp(s - m)
            denom = jnp.sum(e, axis=-1, keepdims=True) + jnp.exp(sink - m)
            o2 = jnp.dot(e.astype(BF16), vv, preferred_element_type=F32) * (1.0 / denom)
            o_ref[:, sl] = jnp.where(low_q, o2[:c], o2[c:]).astype(BF16)


def _attention(aq, ak, av, sinks, caps, batch, seq_len):
    t = aq.shape[0]
    c = ATTN_BLOCK
    n = seq_len // c
    cur = lambda b, i: (b * n + i, 0)
    prev = lambda b, i: (b * n + jnp.maximum(i - 1, 0), 0)
    return pl.pallas_call(
        _attn_kernel,
        grid=(batch, n),
        in_specs=[
            pl.BlockSpec(memory_space=pltpu.SMEM),
            pl.BlockSpec((c, ATTN_Q), cur),
            pl.BlockSpec((c, ATTN_KV), prev),
            pl.BlockSpec((c, ATTN_KV), cur),
            pl.BlockSpec((c, ATTN_KV), prev),
            pl.BlockSpec((c, ATTN_KV), cur),
            _resident((2, 2 * c, 2 * c)),
        ],
        out_specs=pl.BlockSpec((c, ATTN_Q), cur),
        out_shape=jax.ShapeDtypeStruct((t, ATTN_Q), BF16),
        compiler_params=pltpu.CompilerParams(
            dimension_semantics=("parallel", "arbitrary"), vmem_limit_bytes=VMEM_LIMIT_BYTES),
        name="swattn",
    )(sinks, aq, ak, ak, av, av, caps)


def _merge_kernel(x_ref, a_ref, b_ref, ga_ref, gb_ref, wr_ref, wa_ref, wo_ref, o_ref):
    branch_a = jnp.dot(a_ref[...], wr_ref[...], preferred_element_type=F32)
    branch_b = jnp.dot(b_ref[...], wa_ref[...], preferred_element_type=F32)
    merged = (_sigmoid(ga_ref[...].astype(F32)) * branch_a
              + _sigmoid(gb_ref[...].astype(F32)) * branch_b)
    o_ref[...] = x_ref[...] + jnp.dot(merged.astype(BF16), wo_ref[...], preferred_element_type=F32)


def _merge(x2d, a, b, ga, gb, w_ret_out, w_attn_out, w_out):
    t = x2d.shape[0]
    tm = TOKEN_TILE
    row = lambda i: (i, 0)
    tile = pl.BlockSpec((tm, D_MODEL), row)
    return pl.pallas_call(
        _merge_kernel,
        grid=(t // tm,),
        in_specs=[tile, tile, tile, tile, tile,
                  _resident((RET_V, D_MODEL)), _resident((ATTN_Q, D_MODEL)),
                  _resident((D_MODEL, D_MODEL))],
        out_specs=tile,
        out_shape=jax.ShapeDtypeStruct((t, D_MODEL), F32),
        compiler_params=pltpu.CompilerParams(
            dimension_semantics=("parallel",), vmem_limit_bytes=VMEM_LIMIT_BYTES),
        name="merge",
    )(x2d, a, b, ga, gb, w_ret_out, w_attn_out, w_out)


def _ffn_kernel(x_ref, g2_ref, wg_ref, wu_ref, wd_ref, gf_ref, o_ref, *, final_norm):
    x = x_ref[...]
    h = _rms_norm(x, g2_ref[...]).astype(BF16)
    gate = jnp.dot(h, wg_ref[...], preferred_element_type=F32)
    up = jnp.dot(h, wu_ref[...], preferred_element_type=F32)
    act = ((gate * _sigmoid(gate)) * up).astype(BF16)
    y = x + jnp.dot(act, wd_ref[...], preferred_element_type=F32)
    o_ref[...] = _rms_norm(y, gf_ref[...]) if final_norm else y


def _ffn(x2d, ln2_g, w_gate, w_up, w_down, lnf_g, final_norm):
    t = x2d.shape[0]
    tm = TOKEN_TILE
    row = lambda i: (i, 0)
    tile = pl.BlockSpec((tm, D_MODEL), row)
    return pl.pallas_call(
        functools.partial(_ffn_kernel, final_norm=final_norm),
        grid=(t // tm,),
        in_specs=[tile, _resident((1, D_MODEL)),
                  _resident((D_MODEL, D_FF)), _resident((D_MODEL, D_FF)),
                  _resident((D_FF, D_MODEL)), _resident((1, D_MODEL))],
        out_specs=tile,
        out_shape=jax.ShapeDtypeStruct((t, D_MODEL), F32),
        compiler_params=pltpu.CompilerParams(
            dimension_semantics=("parallel",), vmem_limit_bytes=VMEM_LIMIT_BYTES),
        name="ffn",
    )(x2d, ln2_g, w_gate, w_up, w_down, lnf_g)


def _rotary_tables(seq_len):
    pos = jnp.arange(seq_len, dtype=jnp.int32).astype(F32)

    def cos_sin(dim):
        half = dim // 2
        inv_freq = ROPE_THETA ** (-jnp.arange(half, dtype=F32) / half)
        ang = pos[:, None] * inv_freq[None, :]
        cos, sin = jnp.cos(ang), jnp.sin(ang)
        reps = LANES // dim
        return (jnp.tile(jnp.concatenate([cos, cos], axis=-1), (1, reps)),
                jnp.tile(jnp.concatenate([-sin, sin], axis=-1), (1, reps)))

    return cos_sin(RET_QK_DIM) + cos_sin(ATTN_HEAD_DIM)


def _decay_tables():
    c = RET_CHUNK
    log_gamma = jnp.log1p(-jnp.exp2(-5.0 - jnp.arange(RET_HEADS, dtype=F32)))
    idx = jnp.arange(c, dtype=F32)
    rel = idx[:, None] - idx[None, :]
    intra = jnp.where(rel[None] >= 0,
                      jnp.exp(log_gamma[:, None, None] * jnp.maximum(rel, 0.0)[None]), 0.0)
    q_decay = jnp.exp(log_gamma[:, None] * (idx + 1.0))[:, :, None]
    k_decay = jnp.exp(log_gamma[:, None] * (c - 1.0 - idx))[:, :, None]
    chunk_decay = jnp.exp(log_gamma * c)[:, None, None]
    return (intra,
            jnp.broadcast_to(q_decay, (RET_HEADS, c, RET_V_DIM)),
            jnp.broadcast_to(k_decay, (RET_HEADS, c, RET_QK_DIM)),
            jnp.broadcast_to(chunk_decay, (RET_HEADS, 1, RET_V_DIM)))


def _score_caps():
    c = ATTN_BLOCK
    qi = jnp.arange(c)[:, None]
    kj = jnp.arange(2 * c)[None, :]
    rel = c + qi - kj
    band = (rel >= 0) & (rel < WINDOW)
    inf = jnp.float32(jnp.inf)
    later = jnp.where(band, inf, MASK_VALUE)
    first = jnp.where(band & (kj >= c), inf, MASK_VALUE)
    return jnp.stack([jnp.tile(first, (2, 1)), jnp.tile(later, (2, 1))]).astype(F32)


def kernel(x, ln1_g, w_in, b_in, ret_norm_g, w_ret_out, attn_sinks, w_attn_out, w_out,
           ln2_g, w_ffn_gate, w_ffn_up, w_ffn_down, lnf_g):
    batch, seq_len, d_model = x.shape
    depth = w_in.shape[0]
    assert d_model == D_MODEL and w_in.shape[2] == D_IN
    assert seq_len % TOKEN_TILE == 0 and seq_len % RET_CHUNK == 0 and seq_len % ATTN_BLOCK == 0

    tables = _rotary_tables(seq_len)
    decays = _decay_tables()
    caps = _score_caps()
    xs = x.reshape(batch * seq_len, d_model)
    for l in range(depth):
        rq, rk, rv, rg, aq, ak, av, ga, gb = _inproj(
            xs, ln1_g[l][None], w_in[l].astype(BF16), b_in[l][None], tables, seq_len)
        a = _retention(rq, rk, rv, rg, decays, ret_norm_g[l][None], batch, seq_len)
        b = _attention(aq, ak, av, attn_sinks[l], caps, batch, seq_len)
        xs = _merge(xs, a, b, ga, gb, w_ret_out[l].astype(BF16), w_attn_out[l].astype(BF16),
                    w_out[l].astype(BF16))
        xs = _ffn(xs, ln2_g[l][None], w_ffn_gate[l].astype(BF16), w_ffn_up[l].astype(BF16),
                  w_ffn_down[l].astype(BF16), lnf_g[None], final_norm=(l == depth - 1))
    return xs.reshape(batch, seq_len, d_model)
```

```python
import functools

import jax
import jax.numpy as jnp
from jax import lax
from jax.experimental import pallas as pl
from jax.experimental.pallas import tpu as pltpu

F32 = jnp.float32
BF16 = jnp.bfloat16

D_MODEL = 1024
RET_HEADS = 4
RET_QK_DIM = 128
RET_V_DIM = 256
RET_CHUNK = 128
ATTN_Q_HEADS = 16
ATTN_KV_HEADS = 2
ATTN_HEAD_DIM = 64
WINDOW = 128
ATTN_BLOCK = 128
D_FF = 2816
ROPE_THETA = 10000.0
EPS = 1e-6
MASK_VALUE = -1e30

RET_QK = RET_HEADS * RET_QK_DIM
RET_V = RET_HEADS * RET_V_DIM
ATTN_Q = ATTN_Q_HEADS * ATTN_HEAD_DIM
ATTN_KV = ATTN_KV_HEADS * ATTN_HEAD_DIM
OFF_RQ = 0
OFF_RK = OFF_RQ + RET_QK
OFF_RV = OFF_RK + RET_QK
OFF_RG = OFF_RV + RET_V
OFF_AQ = OFF_RG + RET_V
OFF_AK = OFF_AQ + ATTN_Q
OFF_AV = OFF_AK + ATTN_KV
OFF_GA = OFF_AV + ATTN_KV
OFF_GB = OFF_GA + D_MODEL
D_IN = OFF_GB + D_MODEL

LANES = 128
TOKEN_TILE = 512
RET_STEP_CHUNKS = TOKEN_TILE // RET_CHUNK
VMEM_LIMIT_BYTES = 56 * 1024 * 1024


def _sigmoid(x):
    return 1.0 / (1.0 + jnp.exp(-x))


def _rms_norm(x, g):
    ms = jnp.mean(x * x, axis=-1, keepdims=True)
    return (x * lax.rsqrt(ms + EPS)) * g


def _resident(shape):
    nd = len(shape)
    return pl.BlockSpec(shape, lambda *_: (0,) * nd, pipeline_mode=pl.Buffered(1))


def _rot_half128(x, cos, sin_signed):
    return x * cos + pltpu.roll(x, 64, 1) * sin_signed


def _rot_half64(x, cos, sin_signed, first_half):
    partner = jnp.where(first_half, pltpu.roll(x, 96, 1), pltpu.roll(x, 32, 1))
    return x * cos + partner * sin_signed


def _inproj_kernel(x_ref, g_ref, w_ref, b_ref, cr_ref, sr_ref, ca_ref, sa_ref, qdec_ref, kdec_ref,
                   rq_ref, rqd_ref, rk_ref, rkd_ref, rv_ref, sg_ref, aq_ref, ak_ref, av_ref,
                   ga_ref, gb_ref):
    h = _rms_norm(x_ref[...], g_ref[...]).astype(BF16)

    def proj(off, width):
        acc = jnp.dot(h, w_ref[:, off:off + width], preferred_element_type=F32)
        return acc + b_ref[:, off:off + width]

    cr, sr = cr_ref[...], sr_ref[...]
    ca, sa = ca_ref[...], sa_ref[...]
    lane = lax.broadcasted_iota(jnp.int32, ca.shape, 1)
    first_half = (lane % ATTN_HEAD_DIM) < (ATTN_HEAD_DIM // 2)

    q_scale = RET_QK_DIM ** -0.5
    rq = proj(OFF_RQ, RET_QK)
    rk = proj(OFF_RK, RET_QK)
    for hd in range(RET_HEADS):
        sl = slice(hd * LANES, (hd + 1) * LANES)
        q = _rot_half128(rq[:, sl], cr, sr) * q_scale
        rq_ref[:, sl] = q.astype(BF16)
        rqd_ref[:, sl] = (q * qdec_ref[:, sl]).astype(BF16)
        k = _rot_half128(rk[:, sl], cr, sr)
        rk_ref[:, sl] = k.astype(BF16)
        rkd_ref[:, sl] = (k * kdec_ref[:, sl]).astype(BF16)
    rv_ref[...] = proj(OFF_RV, RET_V).astype(BF16)
    rg = proj(OFF_RG, RET_V)
    sg_ref[...] = (rg * _sigmoid(rg)).astype(BF16)
    a_scale = ATTN_HEAD_DIM ** -0.5
    aq = proj(OFF_AQ, ATTN_Q)
    for c in range(ATTN_Q // LANES):
        sl = slice(c * LANES, (c + 1) * LANES)
        aq_ref[:, sl] = (_rot_half64(aq[:, sl], ca, sa, first_half) * a_scale).astype(BF16)
    akv = proj(OFF_AK, 2 * ATTN_KV)
    ak_ref[...] = _rot_half64(akv[:, :ATTN_KV], ca, sa, first_half).astype(BF16)
    av_ref[...] = akv[:, ATTN_KV:].astype(BF16)
    ga_ref[...] = proj(OFF_GA, D_MODEL).astype(BF16)
    gb_ref[...] = proj(OFF_GB, D_MODEL).astype(BF16)


def _inproj(x2d, ln_g, w_in, b_in, tables, row_decays, seq_len):
    t = x2d.shape[0]
    tm = TOKEN_TILE
    pos_tiles = seq_len // tm
    row = lambda i: (i, 0)
    pos = lambda i: (i % pos_tiles, 0)
    widths = (RET_QK, RET_QK, RET_QK, RET_QK, RET_V, RET_V, ATTN_Q, ATTN_KV, ATTN_KV,
              D_MODEL, D_MODEL)
    return pl.pallas_call(
        _inproj_kernel,
        grid=(t // tm,),
        in_specs=[
            pl.BlockSpec((tm, D_MODEL), row),
            _resident((1, D_MODEL)),
            _resident((D_MODEL, D_IN)),
            _resident((1, D_IN)),
            pl.BlockSpec((tm, LANES), pos),
            pl.BlockSpec((tm, LANES), pos),
            pl.BlockSpec((tm, LANES), pos),
            pl.BlockSpec((tm, LANES), pos),
            _resident((tm, RET_QK)),
            _resident((tm, RET_QK)),
        ],
        out_specs=[pl.BlockSpec((tm, w), row) for w in widths],
        out_shape=[jax.ShapeDtypeStruct((t, w), BF16) for w in widths],
        compiler_params=pltpu.CompilerParams(
            dimension_semantics=("parallel",), vmem_limit_bytes=VMEM_LIMIT_BYTES),
        name="inproj",
    )(x2d, ln_g, w_in, b_in, *tables, *row_decays)


def _ret_kernel(q_ref, qd_ref, k_ref, kd_ref, v_ref, sg_ref, intra_ref, cdec_ref, gn_ref,
                o_ref, state_ref):
    @pl.when(pl.program_id(1) == 0)
    def _():
        state_ref[...] = jnp.zeros_like(state_ref)

    c = RET_CHUNK
    for hd in range(RET_HEADS):
        qk = slice(hd * RET_QK_DIM, (hd + 1) * RET_QK_DIM)
        vs = slice(hd * RET_V_DIM, (hd + 1) * RET_V_DIM)
        intra = intra_ref[hd]
        cdec = cdec_ref[hd]
        gn = gn_ref[:, vs]
        state = state_ref[hd]
        for ci in range(RET_STEP_CHUNKS):
            rows = slice(ci * c, (ci + 1) * c)
            v = v_ref[rows, vs]
            scores = lax.dot_general(q_ref[rows, qk], k_ref[rows, qk], (((1,), (1,)), ((), ())),
                                     preferred_element_type=F32)
            inner = jnp.dot((scores * intra).astype(BF16), v, preferred_element_type=F32)
            cross = jnp.dot(qd_ref[rows, qk], state.astype(BF16), preferred_element_type=F32)
            update = lax.dot_general(kd_ref[rows, qk], v, (((0,), (0,)), ((), ())),
                                     preferred_element_type=F32)
            state = state * cdec + update

            y = inner + cross
            yc = y - jnp.mean(y, axis=-1, keepdims=True)
            var = jnp.mean(yc * yc, axis=-1, keepdims=True)
            yn = (yc * lax.rsqrt(var + EPS)) * gn
            o_ref[rows, vs] = (sg_ref[rows, vs].astype(F32) * yn).astype(BF16)
        state_ref[hd] = state


def _retention(rq, rqd, rk, rkd, rv, sg, intra, cdec, gn_g, batch, seq_len):
    t = rq.shape[0]
    rows = RET_STEP_CHUNKS * RET_CHUNK
    n = seq_len // rows
    row = lambda b, i: (b * n + i, 0)
    qk_tile = pl.BlockSpec((rows, RET_QK), row)
    v_tile = pl.BlockSpec((rows, RET_V), row)
    return pl.pallas_call(
        _ret_kernel,
        grid=(batch, n),
        in_specs=[
            qk_tile, qk_tile, qk_tile, qk_tile, v_tile, v_tile,
            _resident((RET_HEADS, RET_CHUNK, RET_CHUNK)),
            _resident((RET_HEADS, 1, RET_V_DIM)),
            _resident((1, RET_V)),
        ],
        out_specs=v_tile,
        out_shape=jax.ShapeDtypeStruct((t, RET_V), BF16),
        scratch_shapes=[pltpu.VMEM((RET_HEADS, RET_QK_DIM, RET_V_DIM), F32)],
        compiler_params=pltpu.CompilerParams(
            dimension_semantics=("parallel", "arbitrary"), vmem_limit_bytes=VMEM_LIMIT_BYTES),
        name="retention",
    )(rq, rqd, rk, rkd, rv, sg, intra, cdec, gn_g)


def _attn_kernel(sink_ref, q_ref, kp_ref, kc_ref, vp_ref, vc_ref, cap_ref, o_ref):
    c = ATTN_BLOCK
    first = pl.program_id(1) == 0
    cap = cap_ref[jnp.where(first, 0, 1)]
    low_kv = lax.broadcasted_iota(jnp.int32, (2 * c, LANES), 1) < ATTN_HEAD_DIM
    low_q = lax.broadcasted_iota(jnp.int32, (c, LANES), 1) < ATTN_HEAD_DIM
    top_rows = lax.broadcasted_iota(jnp.int32, (2 * c, 1), 0) < c

    keys = jnp.concatenate([kp_ref[...], kc_ref[...]], axis=0).astype(F32)
    vals = jnp.concatenate([vp_ref[...], vc_ref[...]], axis=0).astype(F32)
    keys_sw = pltpu.roll(keys, ATTN_HEAD_DIM, 1)
    vals_sw = pltpu.roll(vals, ATTN_HEAD_DIM, 1)
    pairs_per_kv = ATTN_Q_HEADS // ATTN_KV_HEADS // 2
    for g in range(ATTN_KV_HEADS):
        if g == 0:
            kk = jnp.where(low_kv, keys, keys_sw).astype(BF16)
            vv = jnp.where(low_kv, vals, vals_sw).astype(BF16)
        else:
            kk = jnp.where(low_kv, keys_sw, keys).astype(BF16)
            vv = jnp.where(low_kv, vals_sw, vals).astype(BF16)
        for j in range(pairs_per_kv):
            p = g * pairs_per_kv + j
            sl = slice(p * LANES, (p + 1) * LANES)
            qp = q_ref[:, sl]
            zero = jnp.zeros_like(qp)
            q2 = jnp.concatenate([jnp.where(low_q, qp, zero), jnp.where(low_q, zero, qp)], axis=0)
            s = lax.dot_general(q2, kk, (((1,), (1,)), ((), ())), preferred_element_type=F32)
            s = jnp.minimum(s, cap)
            sink = jnp.where(top_rows, sink_ref[2 * p], sink_ref[2 * p + 1])
            m = jnp.maximum(jnp.max(s, axis=-1, keepdims=True), sink)
            e = jnp.exp(s - m)
            denom = jnp.sum(e, axis=-1, keepdims=True) + jnp.exp(sink - m)
            o2 = jnp.dot(e.astype(BF16), vv, preferred_element_type=F32) * (1.0 / denom)
            o_ref[:, sl] = jnp.where(low_q, o2[:c], o2[c:]).astype(BF16)


def _attention(aq, ak, av, sinks, caps, batch, seq_len):
    t = aq.shape[0]
    c = ATTN_BLOCK
    n = seq_len // c
    cur = lambda b, i: (b * n + i, 0)
    prev = lambda b, i: (b * n + jnp.maximum(i - 1, 0), 0)
    return pl.pallas_call(
        _attn_kernel,
        grid=(batch, n),
        in_specs=[
            pl.BlockSpec(memory_space=pltpu.SMEM),
            pl.BlockSpec((c, ATTN_Q), cur),
            pl.BlockSpec((c, ATTN_KV), prev),
            pl.BlockSpec((c, ATTN_KV), cur),
            pl.BlockSpec((c, ATTN_KV), prev),
            pl.BlockSpec((c, ATTN_KV), cur),
            _resident((2, 2 * c, 2 * c)),
        ],
        out_specs=pl.BlockSpec((c, ATTN_Q), cur),
        out_shape=jax.ShapeDtypeStruct((t, ATTN_Q), BF16),
        compiler_params=pltpu.CompilerParams(
            dimension_semantics=("parallel", "arbitrary"), vmem_limit_bytes=VMEM_LIMIT_BYTES),
        name="swattn",
    )(sinks, aq, ak, ak, av, av, caps)


def _merge_kernel(x_ref, a_ref, b_ref, ga_ref, gb_ref, wr_ref, wa_ref, wo_ref, o_ref):
    branch_a = jnp.dot(a_ref[...], wr_ref[...], preferred_element_type=F32)
    branch_b = jnp.dot(b_ref[...], wa_ref[...], preferred_element_type=F32)
    merged = (_sigmoid(ga_ref[...].astype(F32)) * branch_a
              + _sigmoid(gb_ref[...].astype(F32)) * branch_b)
    o_ref[...] = x_ref[...] + jnp.dot(merged.astype(BF16), wo_ref[...], preferred_element_type=F32)


def _merge(x2d, a, b, ga, gb, w_ret_out, w_attn_out, w_out):
    t = x2d.shape[0]
    tm = TOKEN_TILE
    row = lambda i: (i, 0)
    tile = pl.BlockSpec((tm, D_MODEL), row)
    return pl.pallas_call(
        _merge_kernel,
        grid=(t // tm,),
        in_specs=[tile, tile, tile, tile, tile,
                  _resident((RET_V, D_MODEL)), _resident((ATTN_Q, D_MODEL)),
                  _resident((D_MODEL, D_MODEL))],
        out_specs=tile,
        out_shape=jax.ShapeDtypeStruct((t, D_MODEL), F32),
        compiler_params=pltpu.CompilerParams(
            dimension_semantics=("parallel",), vmem_limit_bytes=VMEM_LIMIT_BYTES),
        name="merge",
    )(x2d, a, b, ga, gb, w_ret_out, w_attn_out, w_out)


def _ffn_kernel(x_ref, g2_ref, wg_ref, wu_ref, wd_ref, gf_ref, o_ref, *, final_norm):
    x = x_ref[...]
    h = _rms_norm(x, g2_ref[...]).astype(BF16)
    gate = jnp.dot(h, wg_ref[...], preferred_element_type=F32)
    up = jnp.dot(h, wu_ref[...], preferred_element_type=F32)
    act = ((gate * _sigmoid(gate)) * up).astype(BF16)
    y = x + jnp.dot(act, wd_ref[...], preferred_element_type=F32)
    o_ref[...] = _rms_norm(y, gf_ref[...]) if final_norm else y


def _ffn(x2d, ln2_g, w_gate, w_up, w_down, lnf_g, final_norm):
    t = x2d.shape[0]
    tm = TOKEN_TILE
    row = lambda i: (i, 0)
    tile = pl.BlockSpec((tm, D_MODEL), row)
    return pl.pallas_call(
        functools.partial(_ffn_kernel, final_norm=final_norm),
        grid=(t // tm,),
        in_specs=[tile, _resident((1, D_MODEL)),
                  _resident((D_MODEL, D_FF)), _resident((D_MODEL, D_FF)),
                  _resident((D_FF, D_MODEL)), _resident((1, D_MODEL))],
        out_specs=tile,
        out_shape=jax.ShapeDtypeStruct((t, D_MODEL), F32),
        compiler_params=pltpu.CompilerParams(
            dimension_semantics=("parallel",), vmem_limit_bytes=VMEM_LIMIT_BYTES),
        name="ffn",
    )(x2d, ln2_g, w_gate, w_up, w_down, lnf_g)


def _rotary_tables(seq_len):
    pos = jnp.arange(seq_len, dtype=jnp.int32).astype(F32)

    def cos_sin(dim):
        half = dim // 2
        inv_freq = ROPE_THETA ** (-jnp.arange(half, dtype=F32) / half)
        ang = pos[:, None] * inv_freq[None, :]
        cos, sin = jnp.cos(ang), jnp.sin(ang)
        reps = LANES // dim
        return (jnp.tile(jnp.concatenate([cos, cos], axis=-1), (1, reps)),
                jnp.tile(jnp.concatenate([-sin, sin], axis=-1), (1, reps)))

    return cos_sin(RET_QK_DIM) + cos_sin(ATTN_HEAD_DIM)


def _decay_tables():
    c = RET_CHUNK
    log_gamma = jnp.log1p(-jnp.exp2(-5.0 - jnp.arange(RET_HEADS, dtype=F32)))
    idx = jnp.arange(c, dtype=F32)
    rel = idx[:, None] - idx[None, :]
    intra = jnp.where(rel[None] >= 0,
                      jnp.exp(log_gamma[:, None, None] * jnp.maximum(rel, 0.0)[None]), 0.0)
    q_decay = jnp.exp(log_gamma[:, None] * (idx + 1.0))
    k_decay = jnp.exp(log_gamma[:, None] * (c - 1.0 - idx))
    chunk_decay = jnp.exp(log_gamma * c)[:, None, None]

    def per_row(d):
        lanes = jnp.repeat(d.T, RET_QK_DIM, axis=1)
        return jnp.tile(lanes, (RET_STEP_CHUNKS, 1))

    return (intra, jnp.broadcast_to(chunk_decay, (RET_HEADS, 1, RET_V_DIM)),
            per_row(q_decay), per_row(k_decay))


def _score_caps():
    c = ATTN_BLOCK
    qi = jnp.arange(c)[:, None]
    kj = jnp.arange(2 * c)[None, :]
    rel = c + qi - kj
    band = (rel >= 0) & (rel < WINDOW)
    inf = jnp.float32(jnp.inf)
    later = jnp.where(band, inf, MASK_VALUE)
    first = jnp.where(band & (kj >= c), inf, MASK_VALUE)
    return jnp.stack([jnp.tile(first, (2, 1)), jnp.tile(later, (2, 1))]).astype(F32)


def kernel(x, ln1_g, w_in, b_in, ret_norm_g, w_ret_out, attn_sinks, w_attn_out, w_out,
           ln2_g, w_ffn_gate, w_ffn_up, w_ffn_down, lnf_g):
    batch, seq_len, d_model = x.shape
    depth = w_in.shape[0]
    assert d_model == D_MODEL and w_in.shape[2] == D_IN
    assert seq_len % TOKEN_TILE == 0 and seq_len % ATTN_BLOCK == 0

    tables = _rotary_tables(seq_len)
    intra, cdec, qdec_rows, kdec_rows = _decay_tables()
    caps = _score_caps()
    xs = x.reshape(batch * seq_len, d_model)
    for l in range(depth):
        rq, rqd, rk, rkd, rv, sg, aq, ak, av, ga, gb = _inproj(
            xs, ln1_g[l][None], w_in[l].astype(BF16), b_in[l][None], tables,
            (qdec_rows, kdec_rows), seq_len)
        a = _retention(rq, rqd, rk, rkd, rv, sg, intra, cdec, ret_norm_g[l][None], batch, seq_len)
        b = _attention(aq, ak, av, attn_sinks[l], caps, batch, seq_len)
        xs = _merge(xs, a, b, ga, gb, w_ret_out[l].astype(BF16), w_attn_out[l].astype(BF16),
                    w_out[l].astype(BF16))
        xs = _ffn(xs, ln2_g[l][None], w_ffn_gate[l].astype(BF16), w_ffn_up[l].astype(BF16),
                  w_ffn_down[l].astype(BF16), lnf_g[None], final_norm=(l == depth - 1))
    return xs.reshape(batch, seq_len, d_model)
```

```python
import functools
import math

import jax
import jax.numpy as jnp
from jax import lax
from jax.experimental import pallas as pl
from jax.experimental.pallas import tpu as pltpu

F32 = jnp.float32
BF16 = jnp.bfloat16

D_MODEL = 1024
RET_HEADS = 4
RET_QK_DIM = 128
RET_V_DIM = 256
RET_CHUNK = 128
ATTN_Q_HEADS = 16
ATTN_KV_HEADS = 2
ATTN_HEAD_DIM = 64
WINDOW = 128
ATTN_BLOCK = 128
D_FF = 2816
ROPE_THETA = 10000.0
EPS = 1e-6
MASK_VALUE = -1e30

RET_QK = RET_HEADS * RET_QK_DIM
RET_V = RET_HEADS * RET_V_DIM
ATTN_Q = ATTN_Q_HEADS * ATTN_HEAD_DIM
ATTN_KV = ATTN_KV_HEADS * ATTN_HEAD_DIM
OFF_RQ = 0
OFF_RK = OFF_RQ + RET_QK
OFF_RV = OFF_RK + RET_QK
OFF_RG = OFF_RV + RET_V
OFF_AQ = OFF_RG + RET_V
OFF_AK = OFF_AQ + ATTN_Q
OFF_AV = OFF_AK + ATTN_KV
OFF_GA = OFF_AV + ATTN_KV
OFF_GB = OFF_GA + D_MODEL
D_IN = OFF_GB + D_MODEL

LANES = 128
TOKEN_TILE = 512
RET_STEP_CHUNKS = TOKEN_TILE // RET_CHUNK
ATTN_STEP_BLOCKS = TOKEN_TILE // ATTN_BLOCK
LOG2E = math.log2(math.e)
VMEM_LIMIT_BYTES = 56 * 1024 * 1024


def _sigmoid(x):
    return 1.0 / (1.0 + jnp.exp(-x))


def _rms_norm(x, g):
    ms = jnp.mean(x * x, axis=-1, keepdims=True)
    return (x * lax.rsqrt(ms + EPS)) * g


def _resident(shape):
    nd = len(shape)
    return pl.BlockSpec(shape, lambda *_: (0,) * nd, pipeline_mode=pl.Buffered(1))


def _rot_half128(x, cos, sin_signed):
    return x * cos + pltpu.roll(x, 64, 1) * sin_signed


def _rot_half64(x, cos, sin_signed, first_half):
    partner = jnp.where(first_half, pltpu.roll(x, 96, 1), pltpu.roll(x, 32, 1))
    return x * cos + partner * sin_signed


def _inproj_kernel(x_ref, g_ref, w_ref, b_ref, cr_ref, sr_ref, ca_ref, sa_ref, qdec_ref, kdec_ref,
                   rq_ref, rqd_ref, rk_ref, rkd_ref, rv_ref, sg_ref, aq_ref, ak_ref, av_ref,
                   ga_ref, gb_ref):
    h = _rms_norm(x_ref[...], g_ref[...]).astype(BF16)

    def proj(off, width):
        acc = jnp.dot(h, w_ref[:, off:off + width], preferred_element_type=F32)
        return acc + b_ref[:, off:off + width]

    cr, sr = cr_ref[...], sr_ref[...]
    ca, sa = ca_ref[...], sa_ref[...]
    lane = lax.broadcasted_iota(jnp.int32, ca.shape, 1)
    first_half = (lane % ATTN_HEAD_DIM) < (ATTN_HEAD_DIM // 2)

    q_scale = RET_QK_DIM ** -0.5
    rq = proj(OFF_RQ, RET_QK)
    rk = proj(OFF_RK, RET_QK)
    for hd in range(RET_HEADS):
        sl = slice(hd * LANES, (hd + 1) * LANES)
        q = _rot_half128(rq[:, sl], cr, sr) * q_scale
        rq_ref[:, sl] = q.astype(BF16)
        rqd_ref[:, sl] = (q * qdec_ref[:, sl]).astype(BF16)
        k = _rot_half128(rk[:, sl], cr, sr)
        rk_ref[:, sl] = k.astype(BF16)
        rkd_ref[:, sl] = (k * kdec_ref[:, sl]).astype(BF16)
    rv_ref[...] = proj(OFF_RV, RET_V).astype(BF16)
    rg = proj(OFF_RG, RET_V)
    sg_ref[...] = (rg * _sigmoid(rg)).astype(BF16)
    a_scale = LOG2E * ATTN_HEAD_DIM ** -0.5
    aq = proj(OFF_AQ, ATTN_Q)
    for c in range(ATTN_Q // LANES):
        sl = slice(c * LANES, (c + 1) * LANES)
        aq_ref[:, sl] = (_rot_half64(aq[:, sl], ca, sa, first_half) * a_scale).astype(BF16)
    akv = proj(OFF_AK, 2 * ATTN_KV)
    ak_ref[...] = _rot_half64(akv[:, :ATTN_KV], ca, sa, first_half).astype(BF16)
    av_ref[...] = akv[:, ATTN_KV:].astype(BF16)
    ga_ref[...] = proj(OFF_GA, D_MODEL).astype(BF16)
    gb_ref[...] = proj(OFF_GB, D_MODEL).astype(BF16)


def _inproj(x2d, ln_g, w_in, b_in, tables, row_decays, seq_len):
    t = x2d.shape[0]
    tm = TOKEN_TILE
    pos_tiles = seq_len // tm
    row = lambda i: (i, 0)
    pos = lambda i: (i % pos_tiles, 0)
    widths = (RET_QK, RET_QK, RET_QK, RET_QK, RET_V, RET_V, ATTN_Q, ATTN_KV, ATTN_KV,
              D_MODEL, D_MODEL)
    return pl.pallas_call(
        _inproj_kernel,
        grid=(t // tm,),
        in_specs=[
            pl.BlockSpec((tm, D_MODEL), row),
            _resident((1, D_MODEL)),
            _resident((D_MODEL, D_IN)),
            _resident((1, D_IN)),
            pl.BlockSpec((tm, LANES), pos),
            pl.BlockSpec((tm, LANES), pos),
            pl.BlockSpec((tm, LANES), pos),
            pl.BlockSpec((tm, LANES), pos),
            _resident((tm, RET_QK)),
            _resident((tm, RET_QK)),
        ],
        out_specs=[pl.BlockSpec((tm, w), row) for w in widths],
        out_shape=[jax.ShapeDtypeStruct((t, w), BF16) for w in widths],
        compiler_params=pltpu.CompilerParams(
            dimension_semantics=("parallel",), vmem_limit_bytes=VMEM_LIMIT_BYTES),
        name="inproj",
    )(x2d, ln_g, w_in, b_in, *tables, *row_decays)


def _ret_kernel(q_ref, qd_ref, k_ref, kd_ref, v_ref, sg_ref, intra_ref, cdec_ref, gn_ref,
                o_ref, state_ref):
    @pl.when(pl.program_id(1) == 0)
    def _():
        state_ref[...] = jnp.zeros_like(state_ref)

    c = RET_CHUNK
    for hd in range(RET_HEADS):
        qk = slice(hd * RET_QK_DIM, (hd + 1) * RET_QK_DIM)
        vs = slice(hd * RET_V_DIM, (hd + 1) * RET_V_DIM)
        intra = intra_ref[hd]
        cdec = cdec_ref[hd]
        gn = gn_ref[:, vs]
        state = state_ref[hd]
        for ci in range(RET_STEP_CHUNKS):
            rows = slice(ci * c, (ci + 1) * c)
            v = v_ref[rows, vs]
            scores = lax.dot_general(q_ref[rows, qk], k_ref[rows, qk], (((1,), (1,)), ((), ())),
                                     preferred_element_type=F32)
            inner = jnp.dot((scores * intra).astype(BF16), v, preferred_element_type=F32)
            cross = jnp.dot(qd_ref[rows, qk], state.astype(BF16), preferred_element_type=F32)
            update = lax.dot_general(kd_ref[rows, qk], v, (((0,), (0,)), ((), ())),
                                     preferred_element_type=F32)
            state = state * cdec + update

            y = inner + cross
            yc = y - jnp.mean(y, axis=-1, keepdims=True)
            var = jnp.mean(yc * yc, axis=-1, keepdims=True)
            yn = (yc * lax.rsqrt(var + EPS)) * gn
            o_ref[rows, vs] = (sg_ref[rows, vs].astype(F32) * yn).astype(BF16)
        state_ref[hd] = state


def _retention(rq, rqd, rk, rkd, rv, sg, intra, cdec, gn_g, batch, seq_len):
    t = rq.shape[0]
    rows = RET_STEP_CHUNKS * RET_CHUNK
    n = seq_len // rows
    row = lambda b, i: (b * n + i, 0)
    qk_tile = pl.BlockSpec((rows, RET_QK), row)
    v_tile = pl.BlockSpec((rows, RET_V), row)
    return pl.pallas_call(
        _ret_kernel,
        grid=(batch, n),
        in_specs=[
            qk_tile, qk_tile, qk_tile, qk_tile, v_tile, v_tile,
            _resident((RET_HEADS, RET_CHUNK, RET_CHUNK)),
            _resident((RET_HEADS, 1, RET_V_DIM)),
            _resident((1, RET_V)),
        ],
        out_specs=v_tile,
        out_shape=jax.ShapeDtypeStruct((t, RET_V), BF16),
        scratch_shapes=[pltpu.VMEM((RET_HEADS, RET_QK_DIM, RET_V_DIM), F32)],
        compiler_params=pltpu.CompilerParams(
            dimension_semantics=("parallel", "arbitrary"), vmem_limit_bytes=VMEM_LIMIT_BYTES),
        name="retention",
    )(rq, rqd, rk, rkd, rv, sg, intra, cdec, gn_g)


def _attn_kernel(sink_ref, q_ref, kp_ref, kc_ref, vp_ref, vc_ref, cap_ref, o_ref):
    c = ATTN_BLOCK
    kv_rows = (ATTN_STEP_BLOCKS + 1) * c
    first = pl.program_id(1) == 0
    low_kv = lax.broadcasted_iota(jnp.int32, (kv_rows, LANES), 1) < ATTN_HEAD_DIM
    low_q = lax.broadcasted_iota(jnp.int32, (c, LANES), 1) < ATTN_HEAD_DIM
    eye = (lax.broadcasted_iota(jnp.int32, (c, c), 0) == lax.broadcasted_iota(jnp.int32, (c, c), 1))
    eye2 = jnp.concatenate([eye, eye], axis=0)

    keys = jnp.concatenate([kp_ref[...], kc_ref[...]], axis=0).astype(F32)
    vals = jnp.concatenate([vp_ref[...], vc_ref[...]], axis=0).astype(F32)
    keys_sw = pltpu.roll(keys, ATTN_HEAD_DIM, 1)
    vals_sw = pltpu.roll(vals, ATTN_HEAD_DIM, 1)
    pairs_per_kv = ATTN_Q_HEADS // ATTN_KV_HEADS // 2
    neg_inf = jnp.float32(-jnp.inf)
    for g in range(ATTN_KV_HEADS):
        if g == 0:
            kk_all = jnp.where(low_kv, keys, keys_sw).astype(BF16)
            vv_all = jnp.where(low_kv, vals, vals_sw).astype(BF16)
        else:
            kk_all = jnp.where(low_kv, keys_sw, keys).astype(BF16)
            vv_all = jnp.where(low_kv, vals_sw, vals).astype(BF16)
        for j in range(pairs_per_kv):
            p = g * pairs_per_kv + j
            sl = slice(p * LANES, (p + 1) * LANES)
            fill = jnp.concatenate(
                [jnp.where(eye, sink_ref[2 * p] * LOG2E, neg_inf),
                 jnp.where(eye, sink_ref[2 * p + 1] * LOG2E, neg_inf)], axis=0)
            for blk in range(ATTN_STEP_BLOCKS):
                rows = slice(blk * c, (blk + 1) * c)
                cap = cap_ref[jnp.where(first, 0, 1)] if blk == 0 else cap_ref[1]
                kk = kk_all[blk * c:(blk + 2) * c]
                vv = vv_all[blk * c:(blk + 2) * c]
                qp = q_ref[rows, sl]
                zero = jnp.zeros_like(qp)
                q2 = jnp.concatenate([jnp.where(low_q, qp, zero), jnp.where(low_q, zero, qp)],
                                     axis=0)
                s = lax.dot_general(q2, kk, (((1,), (1,)), ((), ())), preferred_element_type=F32)
                sp = jnp.maximum(jnp.minimum(s[:, :c], cap[:, :c]), fill)
                sc = jnp.minimum(s[:, c:], cap[:, c:])
                m = jnp.max(jnp.maximum(sp, sc), axis=-1, keepdims=True)
                ep = jnp.exp2(sp - m)
                ec = jnp.exp2(sc - m)
                denom = jnp.sum(ep + ec, axis=-1, keepdims=True)
                e = jnp.concatenate([jnp.where(eye2, 0.0, ep), ec], axis=1).astype(BF16)
                o2 = jnp.dot(e, vv, preferred_element_type=F32) * (1.0 / denom)
                o_ref[rows, sl] = jnp.where(low_q, o2[:c], o2[c:]).astype(BF16)


def _attention(aq, ak, av, sinks, caps, batch, seq_len):
    t = aq.shape[0]
    c = ATTN_BLOCK
    rows = ATTN_STEP_BLOCKS * c
    n = seq_len // rows
    cur = lambda b, i: (b * n + i, 0)
    prev = lambda b, i: ((b * n + i) * ATTN_STEP_BLOCKS - jnp.where(i == 0, 0, 1), 0)
    return pl.pallas_call(
        _attn_kernel,
        grid=(batch, n),
        in_specs=[
            pl.BlockSpec(memory_space=pltpu.SMEM),
            pl.BlockSpec((rows, ATTN_Q), cur),
            pl.BlockSpec((c, ATTN_KV), prev),
            pl.BlockSpec((rows, ATTN_KV), cur),
            pl.BlockSpec((c, ATTN_KV), prev),
            pl.BlockSpec((rows, ATTN_KV), cur),
            _resident((2, 2 * c, 2 * c)),
        ],
        out_specs=pl.BlockSpec((rows, ATTN_Q), cur),
        out_shape=jax.ShapeDtypeStruct((t, ATTN_Q), BF16),
        compiler_params=pltpu.CompilerParams(
            dimension_semantics=("parallel", "arbitrary"), vmem_limit_bytes=VMEM_LIMIT_BYTES),
        name="swattn",
    )(sinks, aq, ak, ak, av, av, caps)


def _merge_kernel(x_ref, a_ref, b_ref, ga_ref, gb_ref, wr_ref, wa_ref, wo_ref, o_ref):
    branch_a = jnp.dot(a_ref[...], wr_ref[...], preferred_element_type=F32)
    branch_b = jnp.dot(b_ref[...], wa_ref[...], preferred_element_type=F32)
    merged = (_sigmoid(ga_ref[...].astype(F32)) * branch_a
              + _sigmoid(gb_ref[...].astype(F32)) * branch_b)
    o_ref[...] = x_ref[...] + jnp.dot(merged.astype(BF16), wo_ref[...], preferred_element_type=F32)


def _merge(x2d, a, b, ga, gb, w_ret_out, w_attn_out, w_out):
    t = x2d.shape[0]
    tm = TOKEN_TILE
    row = lambda i: (i, 0)
    tile = pl.BlockSpec((tm, D_MODEL), row)
    return pl.pallas_call(
        _merge_kernel,
        grid=(t // tm,),
        in_specs=[tile, tile, tile, tile, tile,
                  _resident((RET_V, D_MODEL)), _resident((ATTN_Q, D_MODEL)),
                  _resident((D_MODEL, D_MODEL))],
        out_specs=tile,
        out_shape=jax.ShapeDtypeStruct((t, D_MODEL), F32),
        compiler_params=pltpu.CompilerParams(
            dimension_semantics=("parallel",), vmem_limit_bytes=VMEM_LIMIT_BYTES),
        name="merge",
    )(x2d, a, b, ga, gb, w_ret_out, w_attn_out, w_out)


def _ffn_kernel(x_ref, g2_ref, wg_ref, wu_ref, wd_ref, gf_ref, o_ref, *, final_norm):
    x = x_ref[...]
    h = _rms_norm(x, g2_ref[...]).astype(BF16)
    gate = jnp.dot(h, wg_ref[...], preferred_element_type=F32)
    up = jnp.dot(h, wu_ref[...], preferred_element_type=F32)
    act = ((gate * _sigmoid(gate)) * up).astype(BF16)
    y = x + jnp.dot(act, wd_ref[...], preferred_element_type=F32)
    o_ref[...] = _rms_norm(y, gf_ref[...]) if final_norm else y


def _ffn(x2d, ln2_g, w_gate, w_up, w_down, lnf_g, final_norm):
    t = x2d.shape[0]
    tm = TOKEN_TILE
    row = lambda i: (i, 0)
    tile = pl.BlockSpec((tm, D_MODEL), row)
    return pl.pallas_call(
        functools.partial(_ffn_kernel, final_norm=final_norm),
        grid=(t // tm,),
        in_specs=[tile, _resident((1, D_MODEL)),
                  _resident((D_MODEL, D_FF)), _resident((D_MODEL, D_FF)),
                  _resident((D_FF, D_MODEL)), _resident((1, D_MODEL))],
        out_specs=tile,
        out_shape=jax.ShapeDtypeStruct((t, D_MODEL), F32),
        compiler_params=pltpu.CompilerParams(
            dimension_semantics=("parallel",), vmem_limit_bytes=VMEM_LIMIT_BYTES),
        name="ffn",
    )(x2d, ln2_g, w_gate, w_up, w_down, lnf_g)


def _rotary_tables(seq_len):
    pos = jnp.arange(seq_len, dtype=jnp.int32).astype(F32)

    def cos_sin(dim):
        half = dim // 2
        inv_freq = ROPE_THETA ** (-jnp.arange(half, dtype=F32) / half)
        ang = pos[:, None] * inv_freq[None, :]
        cos, sin = jnp.cos(ang), jnp.sin(ang)
        reps = LANES // dim
        return (jnp.tile(jnp.concatenate([cos, cos], axis=-1), (1, reps)),
                jnp.tile(jnp.concatenate([-sin, sin], axis=-1), (1, reps)))

    return cos_sin(RET_QK_DIM) + cos_sin(ATTN_HEAD_DIM)


def _decay_tables():
    c = RET_CHUNK
    log_gamma = jnp.log1p(-jnp.exp2(-5.0 - jnp.arange(RET_HEADS, dtype=F32)))
    idx = jnp.arange(c, dtype=F32)
    rel = idx[:, None] - idx[None, :]
    intra = jnp.where(rel[None] >= 0,
                      jnp.exp(log_gamma[:, None, None] * jnp.maximum(rel, 0.0)[None]), 0.0)
    q_decay = jnp.exp(log_gamma[:, None] * (idx + 1.0))
    k_decay = jnp.exp(log_gamma[:, None] * (c - 1.0 - idx))
    chunk_decay = jnp.exp(log_gamma * c)[:, None, None]

    def per_row(d):
        lanes = jnp.repeat(d.T, RET_QK_DIM, axis=1)
        return jnp.tile(lanes, (RET_STEP_CHUNKS, 1))

    return (intra, jnp.broadcast_to(chunk_decay, (RET_HEADS, 1, RET_V_DIM)),
            per_row(q_decay), per_row(k_decay))


def _score_caps():
    c = ATTN_BLOCK
    qi = jnp.arange(c)[:, None]
    kj = jnp.arange(2 * c)[None, :]
    rel = c + qi - kj
    band = (rel >= 0) & (rel < WINDOW)
    inf = jnp.float32(jnp.inf)
    later = jnp.where(band, inf, MASK_VALUE)
    first = jnp.where(band & (kj >= c), inf, MASK_VALUE)
    return jnp.stack([jnp.tile(first, (2, 1)), jnp.tile(later, (2, 1))]).astype(F32)


def kernel(x, ln1_g, w_in, b_in, ret_norm_g, w_ret_out, attn_sinks, w_attn_out, w_out,
           ln2_g, w_ffn_gate, w_ffn_up, w_ffn_down, lnf_g):
    batch, seq_len, d_model = x.shape
    depth = w_in.shape[0]
    assert d_model == D_MODEL and w_in.shape[2] == D_IN
    assert seq_len % TOKEN_TILE == 0 and seq_len % ATTN_BLOCK == 0

    tables = _rotary_tables(seq_len)
    intra, cdec, qdec_rows, kdec_rows = _decay_tables()
    caps = _score_caps()
    xs = x.reshape(batch * seq_len, d_model)
    for l in range(depth):
        rq, rqd, rk, rkd, rv, sg, aq, ak, av, ga, gb = _inproj(
            xs, ln1_g[l][None], w_in[l].astype(BF16), b_in[l][None], tables,
            (qdec_rows, kdec_rows), seq_len)
        a = _retention(rq, rqd, rk, rkd, rv, sg, intra, cdec, ret_norm_g[l][None], batch, seq_len)
        b = _attention(aq, ak, av, attn_sinks[l], caps, batch, seq_len)
        xs = _merge(xs, a, b, ga, gb, w_ret_out[l].astype(BF16), w_attn_out[l].astype(BF16),
                    w_out[l].astype(BF16))
        xs = _ffn(xs, ln2_g[l][None], w_ffn_gate[l].astype(BF16), w_ffn_up[l].astype(BF16),
                  w_ffn_down[l].astype(BF16), lnf_g[None], final_norm=(l == depth - 1))
    return xs.reshape(batch, seq_len, d_model)
```

```python
import functools
import math

import jax
import jax.numpy as jnp
from jax import lax
from jax.experimental import pallas as pl
from jax.experimental.pallas import tpu as pltpu

F32 = jnp.float32
BF16 = jnp.bfloat16

D_MODEL = 1024
RET_HEADS = 4
RET_QK_DIM = 128
RET_V_DIM = 256
RET_CHUNK = 128
ATTN_Q_HEADS = 16
ATTN_KV_HEADS = 2
ATTN_HEAD_DIM = 64
WINDOW = 128
ATTN_BLOCK = 128
D_FF = 2816
ROPE_THETA = 10000.0
EPS = 1e-6
MASK_VALUE = -1e30

RET_QK = RET_HEADS * RET_QK_DIM
RET_V = RET_HEADS * RET_V_DIM
ATTN_Q = ATTN_Q_HEADS * ATTN_HEAD_DIM
ATTN_KV = ATTN_KV_HEADS * ATTN_HEAD_DIM
OFF_RQ = 0
OFF_RK = OFF_RQ + RET_QK
OFF_RV = OFF_RK + RET_QK
OFF_RG = OFF_RV + RET_V
OFF_AQ = OFF_RG + RET_V
OFF_AK = OFF_AQ + ATTN_Q
OFF_AV = OFF_AK + ATTN_KV
OFF_GA = OFF_AV + ATTN_KV
OFF_GB = OFF_GA + D_MODEL
D_IN = OFF_GB + D_MODEL

LANES = 128
TOKEN_TILE = 512
RET_STEP_CHUNKS = TOKEN_TILE // RET_CHUNK
ATTN_STEP_BLOCKS = TOKEN_TILE // ATTN_BLOCK
LOG2E = math.log2(math.e)
VMEM_LIMIT_BYTES = 56 * 1024 * 1024


def _sigmoid(x):
    return 1.0 / (1.0 + jnp.exp(-x))


def _rms_norm(x, g):
    ms = jnp.mean(x * x, axis=-1, keepdims=True)
    return (x * lax.rsqrt(ms + EPS)) * g


def _resident(shape):
    nd = len(shape)
    return pl.BlockSpec(shape, lambda *_: (0,) * nd, pipeline_mode=pl.Buffered(1))


def _rot_half128(x, cos, sin_signed):
    return x * cos + pltpu.roll(x, 64, 1) * sin_signed


def _rot_half64(x, cos, sin_signed, first_half):
    partner = jnp.where(first_half, pltpu.roll(x, 96, 1), pltpu.roll(x, 32, 1))
    return x * cos + partner * sin_signed


def _inproj_kernel(x_ref, g_ref, w_ref, b_ref, cr_ref, sr_ref, ca_ref, sa_ref, qdec_ref, kdec_ref,
                   rq_ref, rqd_ref, rk_ref, rkd_ref, rv_ref, sg_ref, aq_ref, ak_ref, av_ref,
                   ga_ref, gb_ref):
    h = _rms_norm(x_ref[...], g_ref[...]).astype(BF16)

    def proj(off, width):
        acc = jnp.dot(h, w_ref[:, off:off + width], preferred_element_type=F32)
        return acc + b_ref[:, off:off + width]

    cr, sr = cr_ref[...], sr_ref[...]
    ca, sa = ca_ref[...], sa_ref[...]
    lane = lax.broadcasted_iota(jnp.int32, ca.shape, 1)
    first_half = (lane % ATTN_HEAD_DIM) < (ATTN_HEAD_DIM // 2)

    q_scale = RET_QK_DIM ** -0.5
    rq = proj(OFF_RQ, RET_QK)
    rk = proj(OFF_RK, RET_QK)
    for hd in range(RET_HEADS):
        sl = slice(hd * LANES, (hd + 1) * LANES)
        q = _rot_half128(rq[:, sl], cr, sr) * q_scale
        rq_ref[:, sl] = q.astype(BF16)
        rqd_ref[:, sl] = (q * qdec_ref[:, sl]).astype(BF16)
        k = _rot_half128(rk[:, sl], cr, sr)
        rk_ref[:, sl] = k.astype(BF16)
        rkd_ref[:, sl] = (k * kdec_ref[:, sl]).astype(BF16)
    rv_ref[...] = proj(OFF_RV, RET_V).astype(BF16)
    rg = proj(OFF_RG, RET_V)
    sg_ref[...] = (rg * _sigmoid(rg)).astype(BF16)
    a_scale = LOG2E * ATTN_HEAD_DIM ** -0.5
    aq = proj(OFF_AQ, ATTN_Q)
    for c in range(ATTN_Q // LANES):
        sl = slice(c * LANES, (c + 1) * LANES)
        aq_ref[:, sl] = (_rot_half64(aq[:, sl], ca, sa, first_half) * a_scale).astype(BF16)
    akv = proj(OFF_AK, 2 * ATTN_KV)
    ak_ref[...] = _rot_half64(akv[:, :ATTN_KV], ca, sa, first_half).astype(BF16)
    av_ref[...] = akv[:, ATTN_KV:].astype(BF16)
    ga_ref[...] = proj(OFF_GA, D_MODEL).astype(BF16)
    gb_ref[...] = proj(OFF_GB, D_MODEL).astype(BF16)


def _inproj(x2d, ln_g, w_in, b_in, tables, row_decays, seq_len):
    t = x2d.shape[0]
    tm = TOKEN_TILE
    pos_tiles = seq_len // tm
    row = lambda i: (i, 0)
    pos = lambda i: (i % pos_tiles, 0)
    widths = (RET_QK, RET_QK, RET_QK, RET_QK, RET_V, RET_V, ATTN_Q, ATTN_KV, ATTN_KV,
              D_MODEL, D_MODEL)
    return pl.pallas_call(
        _inproj_kernel,
        grid=(t // tm,),
        in_specs=[
            pl.BlockSpec((tm, D_MODEL), row),
            _resident((1, D_MODEL)),
            _resident((D_MODEL, D_IN)),
            _resident((1, D_IN)),
            pl.BlockSpec((tm, LANES), pos),
            pl.BlockSpec((tm, LANES), pos),
            pl.BlockSpec((tm, LANES), pos),
            pl.BlockSpec((tm, LANES), pos),
            _resident((tm, RET_QK)),
            _resident((tm, RET_QK)),
        ],
        out_specs=[pl.BlockSpec((tm, w), row) for w in widths],
        out_shape=[jax.ShapeDtypeStruct((t, w), BF16) for w in widths],
        compiler_params=pltpu.CompilerParams(
            dimension_semantics=("parallel",), vmem_limit_bytes=VMEM_LIMIT_BYTES),
        name="inproj",
    )(x2d, ln_g, w_in, b_in, *tables, *row_decays)


def _ret_kernel(q_ref, qd_ref, k_ref, kd_ref, v_ref, sg_ref, intra_ref, cdec_ref, gn_ref,
                o_ref, state_ref):
    @pl.when(pl.program_id(1) == 0)
    def _():
        state_ref[...] = jnp.zeros_like(state_ref)

    c = RET_CHUNK
    for hd in range(RET_HEADS):
        qk = slice(hd * RET_QK_DIM, (hd + 1) * RET_QK_DIM)
        vs = slice(hd * RET_V_DIM, (hd + 1) * RET_V_DIM)
        intra = intra_ref[hd]
        cdec = cdec_ref[hd]
        gn = gn_ref[:, vs]
        state = state_ref[hd]
        for ci in range(RET_STEP_CHUNKS):
            rows = slice(ci * c, (ci + 1) * c)
            v = v_ref[rows, vs]
            scores = lax.dot_general(q_ref[rows, qk], k_ref[rows, qk], (((1,), (1,)), ((), ())),
                                     preferred_element_type=F32)
            inner = jnp.dot((scores * intra).astype(BF16), v, preferred_element_type=F32)
            cross = jnp.dot(qd_ref[rows, qk], state.astype(BF16), preferred_element_type=F32)
            update = lax.dot_general(kd_ref[rows, qk], v, (((0,), (0,)), ((), ())),
                                     preferred_element_type=F32)
            state = state * cdec + update

            y = inner + cross
            yc = y - jnp.mean(y, axis=-1, keepdims=True)
            var = jnp.mean(yc * yc, axis=-1, keepdims=True)
            yn = (yc * lax.rsqrt(var + EPS)) * gn
            o_ref[rows, vs] = (sg_ref[rows, vs].astype(F32) * yn).astype(BF16)
        state_ref[hd] = state


def _retention(rq, rqd, rk, rkd, rv, sg, intra, cdec, gn_g, batch, seq_len):
    t = rq.shape[0]
    rows = RET_STEP_CHUNKS * RET_CHUNK
    n = seq_len // rows
    row = lambda b, i: (b * n + i, 0)
    qk_tile = pl.BlockSpec((rows, RET_QK), row)
    v_tile = pl.BlockSpec((rows, RET_V), row)
    return pl.pallas_call(
        _ret_kernel,
        grid=(batch, n),
        in_specs=[
            qk_tile, qk_tile, qk_tile, qk_tile, v_tile, v_tile,
            _resident((RET_HEADS, RET_CHUNK, RET_CHUNK)),
            _resident((RET_HEADS, 1, RET_V_DIM)),
            _resident((1, RET_V)),
        ],
        out_specs=v_tile,
        out_shape=jax.ShapeDtypeStruct((t, RET_V), BF16),
        scratch_shapes=[pltpu.VMEM((RET_HEADS, RET_QK_DIM, RET_V_DIM), F32)],
        compiler_params=pltpu.CompilerParams(
            dimension_semantics=("parallel", "arbitrary"), vmem_limit_bytes=VMEM_LIMIT_BYTES),
        name="retention",
    )(rq, rqd, rk, rkd, rv, sg, intra, cdec, gn_g)


def _attn_kernel(sink_ref, q_ref, kp_ref, kc_ref, vp_ref, vc_ref, cap_ref, o_ref):
    c = ATTN_BLOCK
    kv_rows = (ATTN_STEP_BLOCKS + 1) * c
    first = pl.program_id(1) == 0
    low_kv = lax.broadcasted_iota(jnp.int32, (kv_rows, LANES), 1) < ATTN_HEAD_DIM
    low_q = lax.broadcasted_iota(jnp.int32, (c, LANES), 1) < ATTN_HEAD_DIM
    eye = (lax.broadcasted_iota(jnp.int32, (c, c), 0) == lax.broadcasted_iota(jnp.int32, (c, c), 1))
    eye2 = jnp.concatenate([eye, eye], axis=0)

    keys = jnp.concatenate([kp_ref[...], kc_ref[...]], axis=0).astype(F32)
    vals = jnp.concatenate([vp_ref[...], vc_ref[...]], axis=0).astype(F32)
    keys_sw = pltpu.roll(keys, ATTN_HEAD_DIM, 1)
    vals_sw = pltpu.roll(vals, ATTN_HEAD_DIM, 1)
    pairs_per_kv = ATTN_Q_HEADS // ATTN_KV_HEADS // 2
    neg_inf = jnp.float32(-jnp.inf)
    for g in range(ATTN_KV_HEADS):
        if g == 0:
            kk_all = jnp.where(low_kv, keys, keys_sw).astype(BF16)
            vv_all = jnp.where(low_kv, vals, vals_sw).astype(BF16)
        else:
            kk_all = jnp.where(low_kv, keys_sw, keys).astype(BF16)
            vv_all = jnp.where(low_kv, vals_sw, vals).astype(BF16)
        for j in range(pairs_per_kv):
            p = g * pairs_per_kv + j
            sl = slice(p * LANES, (p + 1) * LANES)
            fill = jnp.concatenate(
                [jnp.where(eye, sink_ref[2 * p] * LOG2E, neg_inf),
                 jnp.where(eye, sink_ref[2 * p + 1] * LOG2E, neg_inf)], axis=0)
            for blk in range(ATTN_STEP_BLOCKS):
                rows = slice(blk * c, (blk + 1) * c)
                cap = cap_ref[jnp.where(first, 0, 1)] if blk == 0 else cap_ref[1]
                kk = kk_all[blk * c:(blk + 2) * c]
                vv = vv_all[blk * c:(blk + 2) * c]
                qp = q_ref[rows, sl]
                zero = jnp.zeros_like(qp)
                q2 = jnp.concatenate([jnp.where(low_q, qp, zero), jnp.where(low_q, zero, qp)],
                                     axis=0)
                s = lax.dot_general(q2, kk, (((1,), (1,)), ((), ())), preferred_element_type=F32)
                sp = jnp.maximum(jnp.minimum(s[:, :c], cap[:, :c]), fill)
                sc = jnp.minimum(s[:, c:], cap[:, c:])
                m = jnp.max(jnp.maximum(sp, sc), axis=-1, keepdims=True)
                ep = jnp.exp2(sp - m)
                ec = jnp.exp2(sc - m)
                denom = jnp.sum(ep + ec, axis=-1, keepdims=True)
                e = jnp.concatenate([jnp.where(eye2, 0.0, ep), ec], axis=1).astype(BF16)
                o2 = jnp.dot(e, vv, preferred_element_type=F32) * (1.0 / denom)
                o_ref[rows, sl] = jnp.where(low_q, o2[:c], o2[c:]).astype(BF16)


def _attention(aq, ak, av, sinks, caps, batch, seq_len):
    t = aq.shape[0]
    c = ATTN_BLOCK
    rows = ATTN_STEP_BLOCKS * c
    n = seq_len // rows
    cur = lambda b, i: (b * n + i, 0)
    prev = lambda b, i: ((b * n + i) * ATTN_STEP_BLOCKS - jnp.where(i == 0, 0, 1), 0)
    return pl.pallas_call(
        _attn_kernel,
        grid=(batch, n),
        in_specs=[
            pl.BlockSpec(memory_space=pltpu.SMEM),
            pl.BlockSpec((rows, ATTN_Q), cur),
            pl.BlockSpec((c, ATTN_KV), prev),
            pl.BlockSpec((rows, ATTN_KV), cur),
            pl.BlockSpec((c, ATTN_KV), prev),
            pl.BlockSpec((rows, ATTN_KV), cur),
            _resident((2, 2 * c, 2 * c)),
        ],
        out_specs=pl.BlockSpec((rows, ATTN_Q), cur),
        out_shape=jax.ShapeDtypeStruct((t, ATTN_Q), BF16),
        compiler_params=pltpu.CompilerParams(
            dimension_semantics=("parallel", "arbitrary"), vmem_limit_bytes=VMEM_LIMIT_BYTES),
        name="swattn",
    )(sinks, aq, ak, ak, av, av, caps)


def _mixers_kernel(sink_ref, rq_ref, rqd_ref, rk_ref, rkd_ref, rv_ref, sg_ref, intra_ref, cdec_ref,
                   gn_ref, aq_ref, kp_ref, kc_ref, vp_ref, vc_ref, cap_ref,
                   a_ref, b_ref, state_ref):
    _ret_kernel(rq_ref, rqd_ref, rk_ref, rkd_ref, rv_ref, sg_ref, intra_ref, cdec_ref, gn_ref,
                a_ref, state_ref)
    _attn_kernel(sink_ref, aq_ref, kp_ref, kc_ref, vp_ref, vc_ref, cap_ref, b_ref)


def _mix_merge_kernel(sink_ref, rq_ref, rqd_ref, rk_ref, rkd_ref, rv_ref, sg_ref, intra_ref,
                      cdec_ref, gn_ref, aq_ref, kp_ref, kc_ref, vp_ref, vc_ref, cap_ref,
                      x_ref, ga_ref, gb_ref, wr_ref, wa_ref, wo_ref,
                      o_ref, state_ref, a_ref, b_ref):
    _mixers_kernel(sink_ref, rq_ref, rqd_ref, rk_ref, rkd_ref, rv_ref, sg_ref, intra_ref, cdec_ref,
                   gn_ref, aq_ref, kp_ref, kc_ref, vp_ref, vc_ref, cap_ref, a_ref, b_ref, state_ref)
    _merge_kernel(x_ref, a_ref, b_ref, ga_ref, gb_ref, wr_ref, wa_ref, wo_ref, o_ref)


def _mix_merge(x2d, rq, rqd, rk, rkd, rv, sg, intra, cdec, gn_g, aq, ak, av, sinks, caps, ga, gb,
               w_ret_out, w_attn_out, w_out, batch, seq_len):
    assert RET_STEP_CHUNKS * RET_CHUNK == ATTN_STEP_BLOCKS * ATTN_BLOCK == TOKEN_TILE
    t = rq.shape[0]
    c = ATTN_BLOCK
    rows = TOKEN_TILE
    n = seq_len // rows
    cur = lambda b, i: (b * n + i, 0)
    prev = lambda b, i: ((b * n + i) * ATTN_STEP_BLOCKS - jnp.where(i == 0, 0, 1), 0)
    qk_tile = pl.BlockSpec((rows, RET_QK), cur)
    wide_tile = pl.BlockSpec((rows, D_MODEL), cur)
    return pl.pallas_call(
        _mix_merge_kernel,
        grid=(batch, n),
        in_specs=[
            pl.BlockSpec(memory_space=pltpu.SMEM),
            qk_tile, qk_tile, qk_tile, qk_tile, wide_tile, wide_tile,
            _resident((RET_HEADS, RET_CHUNK, RET_CHUNK)),
            _resident((RET_HEADS, 1, RET_V_DIM)),
            _resident((1, RET_V)),
            wide_tile,
            pl.BlockSpec((c, ATTN_KV), prev),
            pl.BlockSpec((rows, ATTN_KV), cur),
            pl.BlockSpec((c, ATTN_KV), prev),
            pl.BlockSpec((rows, ATTN_KV), cur),
            _resident((2, 2 * c, 2 * c)),
            wide_tile, wide_tile, wide_tile,
            _resident((RET_V, D_MODEL)), _resident((ATTN_Q, D_MODEL)),
            _resident((D_MODEL, D_MODEL)),
        ],
        out_specs=wide_tile,
        out_shape=jax.ShapeDtypeStruct((t, D_MODEL), F32),
        scratch_shapes=[pltpu.VMEM((RET_HEADS, RET_QK_DIM, RET_V_DIM), F32),
                        pltpu.VMEM((rows, RET_V), BF16), pltpu.VMEM((rows, ATTN_Q), BF16)],
        compiler_params=pltpu.CompilerParams(
            dimension_semantics=("parallel", "arbitrary"), vmem_limit_bytes=VMEM_LIMIT_BYTES),
        name="mix_merge",
    )(sinks, rq, rqd, rk, rkd, rv, sg, intra, cdec, gn_g, aq, ak, ak, av, av, caps,
      x2d, ga, gb, w_ret_out, w_attn_out, w_out)


def _merge_kernel(x_ref, a_ref, b_ref, ga_ref, gb_ref, wr_ref, wa_ref, wo_ref, o_ref):
    branch_a = jnp.dot(a_ref[...], wr_ref[...], preferred_element_type=F32)
    branch_b = jnp.dot(b_ref[...], wa_ref[...], preferred_element_type=F32)
    merged = (_sigmoid(ga_ref[...].astype(F32)) * branch_a
              + _sigmoid(gb_ref[...].astype(F32)) * branch_b)
    o_ref[...] = x_ref[...] + jnp.dot(merged.astype(BF16), wo_ref[...], preferred_element_type=F32)


def _merge(x2d, a, b, ga, gb, w_ret_out, w_attn_out, w_out):
    t = x2d.shape[0]
    tm = TOKEN_TILE
    row = lambda i: (i, 0)
    tile = pl.BlockSpec((tm, D_MODEL), row)
    return pl.pallas_call(
        _merge_kernel,
        grid=(t // tm,),
        in_specs=[tile, tile, tile, tile, tile,
                  _resident((RET_V, D_MODEL)), _resident((ATTN_Q, D_MODEL)),
                  _resident((D_MODEL, D_MODEL))],
        out_specs=tile,
        out_shape=jax.ShapeDtypeStruct((t, D_MODEL), F32),
        compiler_params=pltpu.CompilerParams(
            dimension_semantics=("parallel",), vmem_limit_bytes=VMEM_LIMIT_BYTES),
        name="merge",
    )(x2d, a, b, ga, gb, w_ret_out, w_attn_out, w_out)


def _ffn_kernel(x_ref, g2_ref, wg_ref, wu_ref, wd_ref, gf_ref, o_ref, *, final_norm):
    x = x_ref[...]
    h = _rms_norm(x, g2_ref[...]).astype(BF16)
    gate = jnp.dot(h, wg_ref[...], preferred_element_type=F32)
    up = jnp.dot(h, wu_ref[...], preferred_element_type=F32)
    act = ((gate * _sigmoid(gate)) * up).astype(BF16)
    y = x + jnp.dot(act, wd_ref[...], preferred_element_type=F32)
    o_ref[...] = _rms_norm(y, gf_ref[...]) if final_norm else y


def _ffn(x2d, ln2_g, w_gate, w_up, w_down, lnf_g, final_norm):
    t = x2d.shape[0]
    tm = TOKEN_TILE
    row = lambda i: (i, 0)
    tile = pl.BlockSpec((tm, D_MODEL), row)
    return pl.pallas_call(
        functools.partial(_ffn_kernel, final_norm=final_norm),
        grid=(t // tm,),
        in_specs=[tile, _resident((1, D_MODEL)),
                  _resident((D_MODEL, D_FF)), _resident((D_MODEL, D_FF)),
                  _resident((D_FF, D_MODEL)), _resident((1, D_MODEL))],
        out_specs=tile,
        out_shape=jax.ShapeDtypeStruct((t, D_MODEL), F32),
        compiler_params=pltpu.CompilerParams(
            dimension_semantics=("parallel",), vmem_limit_bytes=VMEM_LIMIT_BYTES),
        name="ffn",
    )(x2d, ln2_g, w_gate, w_up, w_down, lnf_g)


def _rotary_tables(seq_len):
    pos = jnp.arange(seq_len, dtype=jnp.int32).astype(F32)

    def cos_sin(dim):
        half = dim // 2
        inv_freq = ROPE_THETA ** (-jnp.arange(half, dtype=F32) / half)
        ang = pos[:, None] * inv_freq[None, :]
        cos, sin = jnp.cos(ang), jnp.sin(ang)
        reps = LANES // dim
        return (jnp.tile(jnp.concatenate([cos, cos], axis=-1), (1, reps)),
                jnp.tile(jnp.concatenate([-sin, sin], axis=-1), (1, reps)))

    return cos_sin(RET_QK_DIM) + cos_sin(ATTN_HEAD_DIM)


def _decay_tables():
    c = RET_CHUNK
    log_gamma = jnp.log1p(-jnp.exp2(-5.0 - jnp.arange(RET_HEADS, dtype=F32)))
    idx = jnp.arange(c, dtype=F32)
    rel = idx[:, None] - idx[None, :]
    intra = jnp.where(rel[None] >= 0,
                      jnp.exp(log_gamma[:, None, None] * jnp.maximum(rel, 0.0)[None]), 0.0)
    q_decay = jnp.exp(log_gamma[:, None] * (idx + 1.0))
    k_decay = jnp.exp(log_gamma[:, None] * (c - 1.0 - idx))
    chunk_decay = jnp.exp(log_gamma * c)[:, None, None]

    def per_row(d):
        lanes = jnp.repeat(d.T, RET_QK_DIM, axis=1)
        return jnp.tile(lanes, (RET_STEP_CHUNKS, 1))

    return (intra, jnp.broadcast_to(chunk_decay, (RET_HEADS, 1, RET_V_DIM)),
            per_row(q_decay), per_row(k_decay))


def _score_caps():
    c = ATTN_BLOCK
    qi = jnp.arange(c)[:, None]
    kj = jnp.arange(2 * c)[None, :]
    rel = c + qi - kj
    band = (rel >= 0) & (rel < WINDOW)
    inf = jnp.float32(jnp.inf)
    later = jnp.where(band, inf, MASK_VALUE)
    first = jnp.where(band & (kj >= c), inf, MASK_VALUE)
    return jnp.stack([jnp.tile(first, (2, 1)), jnp.tile(later, (2, 1))]).astype(F32)


def kernel(x, ln1_g, w_in, b_in, ret_norm_g, w_ret_out, attn_sinks, w_attn_out, w_out,
           ln2_g, w_ffn_gate, w_ffn_up, w_ffn_down, lnf_g):
    batch, seq_len, d_model = x.shape
    depth = w_in.shape[0]
    assert d_model == D_MODEL and w_in.shape[2] == D_IN
    assert seq_len % TOKEN_TILE == 0 and seq_len % ATTN_BLOCK == 0

    tables = _rotary_tables(seq_len)
    intra, cdec, qdec_rows, kdec_rows = _decay_tables()
    caps = _score_caps()
    xs = x.reshape(batch * seq_len, d_model)
    for l in range(depth):
        rq, rqd, rk, rkd, rv, sg, aq, ak, av, ga, gb = _inproj(
            xs, ln1_g[l][None], w_in[l].astype(BF16), b_in[l][None], tables,
            (qdec_rows, kdec_rows), seq_len)
        xs = _mix_merge(xs, rq, rqd, rk, rkd, rv, sg, intra, cdec, ret_norm_g[l][None],
                        aq, ak, av, attn_sinks[l], caps, ga, gb, w_ret_out[l].astype(BF16),
                        w_attn_out[l].astype(BF16), w_out[l].astype(BF16), batch, seq_len)
        xs = _ffn(xs, ln2_g[l][None], w_ffn_gate[l].astype(BF16), w_ffn_up[l].astype(BF16),
                  w_ffn_down[l].astype(BF16), lnf_g[None], final_norm=(l == depth - 1))
    return xs.reshape(batch, seq_len, d_model)
```

```python
import functools
import math

import jax
import jax.numpy as jnp
import numpy as np
from jax import lax
from jax.experimental import pallas as pl
from jax.experimental.pallas import tpu as pltpu

F32 = jnp.float32
BF16 = jnp.bfloat16

D_MODEL = 1024
RET_HEADS = 4
RET_QK_DIM = 128
RET_V_DIM = 256
RET_CHUNK = 128
ATTN_Q_HEADS = 16
ATTN_KV_HEADS = 2
ATTN_HEAD_DIM = 64
WINDOW = 128
ATTN_BLOCK = 128
D_FF = 2816
ROPE_THETA = 10000.0
EPS = 1e-6
MASK_VALUE = -1e30

RET_QK = RET_HEADS * RET_QK_DIM
RET_V = RET_HEADS * RET_V_DIM
ATTN_Q = ATTN_Q_HEADS * ATTN_HEAD_DIM
ATTN_KV = ATTN_KV_HEADS * ATTN_HEAD_DIM
OFF_RQ = 0
OFF_RK = OFF_RQ + RET_QK
OFF_RV = OFF_RK + RET_QK
OFF_RG = OFF_RV + RET_V
OFF_AQ = OFF_RG + RET_V
OFF_AK = OFF_AQ + ATTN_Q
OFF_AV = OFF_AK + ATTN_KV
OFF_GA = OFF_AV + ATTN_KV
OFF_GB = OFF_GA + D_MODEL
D_IN = OFF_GB + D_MODEL

LANES = 128
TOKEN_TILE = 512
RET_STEP_CHUNKS = TOKEN_TILE // RET_CHUNK
ATTN_STEP_BLOCKS = TOKEN_TILE // ATTN_BLOCK
ATTN_QK_LEAD = 4
LOG2E = math.log2(math.e)
VMEM_LIMIT_BYTES = 56 * 1024 * 1024


def _sigmoid(x):
    return 1.0 / (1.0 + jnp.exp(-x))


def _rms_norm(x, g):
    ms = jnp.mean(x * x, axis=-1, keepdims=True)
    return (x * lax.rsqrt(ms + EPS)) * g


def _resident(shape):
    nd = len(shape)
    return pl.BlockSpec(shape, lambda *_: (0,) * nd, pipeline_mode=pl.Buffered(1))


def _rot_half128(x, cos, sin_signed):
    return x * cos + pltpu.roll(x, 64, 1) * sin_signed


def _rot_half64(x, cos, sin_signed, first_half):
    partner = jnp.where(first_half, pltpu.roll(x, 96, 1), pltpu.roll(x, 32, 1))
    return x * cos + partner * sin_signed


def _inproj_kernel(x_ref, g_ref, w_ref, b_ref, cr_ref, sr_ref, ca_ref, sa_ref, qdec_ref, kdec_ref,
                   rq_ref, rqd_ref, rk_ref, rkd_ref, rv_ref, sg_ref, aq_ref, ak_ref, av_ref,
                   ga_ref, gb_ref):
    h = _rms_norm(x_ref[...], g_ref[...]).astype(BF16)

    def proj(off, width):
        acc = jnp.dot(h, w_ref[:, off:off + width], preferred_element_type=F32)
        return acc + b_ref[:, off:off + width]

    cr, sr = cr_ref[...], sr_ref[...]
    ca, sa = ca_ref[...], sa_ref[...]
    lane = lax.broadcasted_iota(jnp.int32, ca.shape, 1)
    first_half = (lane % ATTN_HEAD_DIM) < (ATTN_HEAD_DIM // 2)

    q_scale = RET_QK_DIM ** -0.5
    rq = proj(OFF_RQ, RET_QK)
    rk = proj(OFF_RK, RET_QK)
    for hd in range(RET_HEADS):
        sl = slice(hd * LANES, (hd + 1) * LANES)
        q = _rot_half128(rq[:, sl], cr, sr) * q_scale
        rq_ref[:, sl] = q.astype(BF16)
        rqd_ref[:, sl] = (q * qdec_ref[:, sl]).astype(BF16)
        k = _rot_half128(rk[:, sl], cr, sr)
        rk_ref[:, sl] = k.astype(BF16)
        rkd_ref[:, sl] = (k * kdec_ref[:, sl]).astype(BF16)
    rv_ref[...] = proj(OFF_RV, RET_V).astype(BF16)
    rg = proj(OFF_RG, RET_V)
    sg_ref[...] = (rg * _sigmoid(rg)).astype(BF16)
    a_scale = LOG2E * ATTN_HEAD_DIM ** -0.5
    aq = proj(OFF_AQ, ATTN_Q)
    for c in range(ATTN_Q // LANES):
        sl = slice(c * LANES, (c + 1) * LANES)
        aq_ref[:, sl] = (_rot_half64(aq[:, sl], ca, sa, first_half) * a_scale).astype(BF16)
    akv = proj(OFF_AK, 2 * ATTN_KV)
    ak_ref[...] = _rot_half64(akv[:, :ATTN_KV], ca, sa, first_half).astype(BF16)
    av_ref[...] = akv[:, ATTN_KV:].astype(BF16)
    ga_ref[...] = proj(OFF_GA, D_MODEL).astype(BF16)
    gb_ref[...] = proj(OFF_GB, D_MODEL).astype(BF16)


def _inproj(x2d, ln_g, w_in, b_in, tables, row_decays, seq_len):
    t = x2d.shape[0]
    tm = TOKEN_TILE
    pos_tiles = seq_len // tm
    row = lambda i: (i, 0)
    pos = lambda i: (i % pos_tiles, 0)
    widths = (RET_QK, RET_QK, RET_QK, RET_QK, RET_V, RET_V, ATTN_Q, ATTN_KV, ATTN_KV,
              D_MODEL, D_MODEL)
    return pl.pallas_call(
        _inproj_kernel,
        grid=(t // tm,),
        in_specs=[
            pl.BlockSpec((tm, D_MODEL), row),
            _resident((1, D_MODEL)),
            _resident((D_MODEL, D_IN)),
            _resident((1, D_IN)),
            pl.BlockSpec((tm, LANES), pos),
            pl.BlockSpec((tm, LANES), pos),
            pl.BlockSpec((tm, LANES), pos),
            pl.BlockSpec((tm, LANES), pos),
            _resident((tm, RET_QK)),
            _resident((tm, RET_QK)),
        ],
        out_specs=[pl.BlockSpec((tm, w), row) for w in widths],
        out_shape=[jax.ShapeDtypeStruct((t, w), BF16) for w in widths],
        compiler_params=pltpu.CompilerParams(
            dimension_semantics=("parallel",), vmem_limit_bytes=VMEM_LIMIT_BYTES),
        name="inproj",
    )(x2d, ln_g, w_in, b_in, *tables, *row_decays)


def _ret_kernel(q_ref, qd_ref, k_ref, kd_ref, v_ref, sg_ref, intra_ref, cdec_ref, gn_ref,
                o_ref, state_ref):
    @pl.when(pl.program_id(1) == 0)
    def _():
        state_ref[...] = jnp.zeros_like(state_ref)

    c = RET_CHUNK
    units = [(hd, ci) for hd in range(RET_HEADS) for ci in range(RET_STEP_CHUNKS)]
    qk_of = lambda hd: slice(hd * RET_QK_DIM, (hd + 1) * RET_QK_DIM)
    vs_of = lambda hd: slice(hd * RET_V_DIM, (hd + 1) * RET_V_DIM)
    rows_of = lambda ci: slice(ci * c, (ci + 1) * c)

    scores, update = {}, {}
    for hd, ci in units:
        rows, qk = rows_of(ci), qk_of(hd)
        scores[hd, ci] = lax.dot_general(q_ref[rows, qk], k_ref[rows, qk],
                                         (((1,), (1,)), ((), ())), preferred_element_type=F32)
        update[hd, ci] = lax.dot_general(kd_ref[rows, qk], v_ref[rows, vs_of(hd)],
                                         (((0,), (0,)), ((), ())), preferred_element_type=F32)

    y = {}
    for hd in range(RET_HEADS):
        state = state_ref[hd]
        for ci in range(RET_STEP_CHUNKS):
            rows = rows_of(ci)
            lhs = jnp.concatenate([(scores[hd, ci] * intra_ref[hd]).astype(BF16),
                                   qd_ref[rows, qk_of(hd)]], axis=1)
            rhs = jnp.concatenate([v_ref[rows, vs_of(hd)], state.astype(BF16)], axis=0)
            y[hd, ci] = jnp.dot(lhs, rhs, preferred_element_type=F32)
            state = state * cdec_ref[hd] + update[hd, ci]
        state_ref[hd] = state

    for hd, ci in units:
        rows, vs = rows_of(ci), vs_of(hd)
        yc = y[hd, ci] - jnp.mean(y[hd, ci], axis=-1, keepdims=True)
        var = jnp.mean(yc * yc, axis=-1, keepdims=True)
        yn = (yc * lax.rsqrt(var + EPS)) * gn_ref[:, vs]
        o_ref[rows, vs] = (sg_ref[rows, vs].astype(F32) * yn).astype(BF16)


def _retention(rq, rqd, rk, rkd, rv, sg, intra, cdec, gn_g, batch, seq_len):
    t = rq.shape[0]
    rows = RET_STEP_CHUNKS * RET_CHUNK
    n = seq_len // rows
    row = lambda b, i: (b * n + i, 0)
    qk_tile = pl.BlockSpec((rows, RET_QK), row)
    v_tile = pl.BlockSpec((rows, RET_V), row)
    return pl.pallas_call(
        _ret_kernel,
        grid=(batch, n),
        in_specs=[
            qk_tile, qk_tile, qk_tile, qk_tile, v_tile, v_tile,
            _resident((RET_HEADS, RET_CHUNK, RET_CHUNK)),
            _resident((RET_HEADS, 1, RET_V_DIM)),
            _resident((1, RET_V)),
        ],
        out_specs=v_tile,
        out_shape=jax.ShapeDtypeStruct((t, RET_V), BF16),
        scratch_shapes=[pltpu.VMEM((RET_HEADS, RET_QK_DIM, RET_V_DIM), F32)],
        compiler_params=pltpu.CompilerParams(
            dimension_semantics=("parallel", "arbitrary"), vmem_limit_bytes=VMEM_LIMIT_BYTES),
        name="retention",
    )(rq, rqd, rk, rkd, rv, sg, intra, cdec, gn_g)


def _attn_kernel(sink_ref, q_ref, kp_ref, kc_ref, vp_ref, vc_ref, cap_ref, o_ref):
    c = ATTN_BLOCK
    kv_rows = (ATTN_STEP_BLOCKS + 1) * c
    first = pl.program_id(1) == 0
    low_kv = lax.broadcasted_iota(jnp.int32, (kv_rows, LANES), 1) < ATTN_HEAD_DIM
    low_q = lax.broadcasted_iota(jnp.int32, (c, LANES), 1) < ATTN_HEAD_DIM
    eye = (lax.broadcasted_iota(jnp.int32, (c, c), 0) == lax.broadcasted_iota(jnp.int32, (c, c), 1))
    eye2 = jnp.concatenate([eye, eye], axis=0)

    keys = jnp.concatenate([kp_ref[...], kc_ref[...]], axis=0).astype(F32)
    vals = jnp.concatenate([vp_ref[...], vc_ref[...]], axis=0).astype(F32)
    keys_sw = pltpu.roll(keys, ATTN_HEAD_DIM, 1)
    vals_sw = pltpu.roll(vals, ATTN_HEAD_DIM, 1)
    pairs_per_kv = ATTN_Q_HEADS // ATTN_KV_HEADS // 2
    neg_inf = jnp.float32(-jnp.inf)
    kk_all = [jnp.where(low_kv, keys, keys_sw).astype(BF16),
              jnp.where(low_kv, keys_sw, keys).astype(BF16)]
    vv_all = [jnp.where(low_kv, vals, vals_sw).astype(BF16),
              jnp.where(low_kv, vals_sw, vals).astype(BF16)]

    def qk_scores(p, blk):
        g = p // pairs_per_kv
        qp = q_ref[blk * c:(blk + 1) * c, p * LANES:(p + 1) * LANES]
        zero = jnp.zeros_like(qp)
        q2 = jnp.concatenate([jnp.where(low_q, qp, zero), jnp.where(low_q, zero, qp)], axis=0)
        return lax.dot_general(q2, kk_all[g][blk * c:(blk + 2) * c], (((1,), (1,)), ((), ())),
                               preferred_element_type=F32)

    def softmax_pv(p, blk, s):
        g = p // pairs_per_kv
        fill = jnp.concatenate(
            [jnp.where(eye, sink_ref[2 * p] * LOG2E, neg_inf),
             jnp.where(eye, sink_ref[2 * p + 1] * LOG2E, neg_inf)], axis=0)
        cap = cap_ref[jnp.where(first, 0, 1)] if blk == 0 else cap_ref[1]
        sp = jnp.maximum(jnp.minimum(s[:, :c], cap[:, :c]), fill)
        sc = jnp.minimum(s[:, c:], cap[:, c:])
        m = jnp.max(jnp.maximum(sp, sc), axis=-1, keepdims=True)
        ep = jnp.exp2(sp - m)
        ec = jnp.exp2(sc - m)
        denom = jnp.sum(ep + ec, axis=-1, keepdims=True)
        e = jnp.concatenate([jnp.where(eye2, 0.0, ep), ec], axis=1).astype(BF16)
        o2 = jnp.dot(e, vv_all[g][blk * c:(blk + 2) * c], preferred_element_type=F32)
        o2 = o2 * (1.0 / denom)
        o_ref[blk * c:(blk + 1) * c, p * LANES:(p + 1) * LANES] = (
            jnp.where(low_q, o2[:c], o2[c:]).astype(BF16))

    units = [(p, blk) for p in range(ATTN_Q_HEADS // 2) for blk in range(ATTN_STEP_BLOCKS)]
    pending = {}
    for i in range(len(units) + ATTN_QK_LEAD):
        if i < len(units):
            pending[i] = qk_scores(*units[i])
        if i >= ATTN_QK_LEAD:
            j = i - ATTN_QK_LEAD
            softmax_pv(*units[j], pending.pop(j))


def _attention(aq, ak, av, sinks, caps, batch, seq_len):
    t = aq.shape[0]
    c = ATTN_BLOCK
    rows = ATTN_STEP_BLOCKS * c
    n = seq_len // rows
    cur = lambda b, i: (b * n + i, 0)
    prev = lambda b, i: ((b * n + i) * ATTN_STEP_BLOCKS - jnp.where(i == 0, 0, 1), 0)
    return pl.pallas_call(
        _attn_kernel,
        grid=(batch, n),
        in_specs=[
            pl.BlockSpec(memory_space=pltpu.SMEM),
            pl.BlockSpec((rows, ATTN_Q), cur),
            pl.BlockSpec((c, ATTN_KV), prev),
            pl.BlockSpec((rows, ATTN_KV), cur),
            pl.BlockSpec((c, ATTN_KV), prev),
            pl.BlockSpec((rows, ATTN_KV), cur),
            _resident((2, 2 * c, 2 * c)),
        ],
        out_specs=pl.BlockSpec((rows, ATTN_Q), cur),
        out_shape=jax.ShapeDtypeStruct((t, ATTN_Q), BF16),
        compiler_params=pltpu.CompilerParams(
            dimension_semantics=("parallel", "arbitrary"), vmem_limit_bytes=VMEM_LIMIT_BYTES),
        name="swattn",
    )(sinks, aq, ak, ak, av, av, caps)


def _mixers_kernel(sink_ref, rq_ref, rqd_ref, rk_ref, rkd_ref, rv_ref, sg_ref, intra_ref, cdec_ref,
                   gn_ref, aq_ref, kp_ref, kc_ref, vp_ref, vc_ref, cap_ref,
                   a_ref, b_ref, state_ref):
    _ret_kernel(rq_ref, rqd_ref, rk_ref, rkd_ref, rv_ref, sg_ref, intra_ref, cdec_ref, gn_ref,
                a_ref, state_ref)
    _attn_kernel(sink_ref, aq_ref, kp_ref, kc_ref, vp_ref, vc_ref, cap_ref, b_ref)


def _mix_merge_kernel(sink_ref, rq_ref, rqd_ref, rk_ref, rkd_ref, rv_ref, sg_ref, intra_ref,
                      cdec_ref, gn_ref, aq_ref, kp_ref, kc_ref, vp_ref, vc_ref, cap_ref,
                      x_ref, ga_ref, gb_ref, wr_ref, wa_ref, wo_ref,
                      o_ref, state_ref, a_ref, b_ref):
    _mixers_kernel(sink_ref, rq_ref, rqd_ref, rk_ref, rkd_ref, rv_ref, sg_ref, intra_ref, cdec_ref,
                   gn_ref, aq_ref, kp_ref, kc_ref, vp_ref, vc_ref, cap_ref, a_ref, b_ref, state_ref)
    _merge_kernel(x_ref, a_ref, b_ref, ga_ref, gb_ref, wr_ref, wa_ref, wo_ref, o_ref)


def _mix_merge(x2d, rq, rqd, rk, rkd, rv, sg, intra, cdec, gn_g, aq, ak, av, sinks, caps, ga, gb,
               w_ret_out, w_attn_out, w_out, batch, seq_len):
    assert RET_STEP_CHUNKS * RET_CHUNK == ATTN_STEP_BLOCKS * ATTN_BLOCK == TOKEN_TILE
    t = rq.shape[0]
    c = ATTN_BLOCK
    rows = TOKEN_TILE
    n = seq_len // rows
    cur = lambda b, i: (b * n + i, 0)
    prev = lambda b, i: ((b * n + i) * ATTN_STEP_BLOCKS - jnp.where(i == 0, 0, 1), 0)
    qk_tile = pl.BlockSpec((rows, RET_QK), cur)
    wide_tile = pl.BlockSpec((rows, D_MODEL), cur)
    return pl.pallas_call(
        _mix_merge_kernel,
        grid=(batch, n),
        in_specs=[
            pl.BlockSpec(memory_space=pltpu.SMEM),
            qk_tile, qk_tile, qk_tile, qk_tile, wide_tile, wide_tile,
            _resident((RET_HEADS, RET_CHUNK, RET_CHUNK)),
            _resident((RET_HEADS, 1, RET_V_DIM)),
            _resident((1, RET_V)),
            wide_tile,
            pl.BlockSpec((c, ATTN_KV), prev),
            pl.BlockSpec((rows, ATTN_KV), cur),
            pl.BlockSpec((c, ATTN_KV), prev),
            pl.BlockSpec((rows, ATTN_KV), cur),
            _resident((2, 2 * c, 2 * c)),
            wide_tile, wide_tile, wide_tile,
            _resident((RET_V, D_MODEL)), _resident((ATTN_Q, D_MODEL)),
            _resident((D_MODEL, D_MODEL)),
        ],
        out_specs=wide_tile,
        out_shape=jax.ShapeDtypeStruct((t, D_MODEL), F32),
        scratch_shapes=[pltpu.VMEM((RET_HEADS, RET_QK_DIM, RET_V_DIM), F32),
                        pltpu.VMEM((rows, RET_V), BF16), pltpu.VMEM((rows, ATTN_Q), BF16)],
        compiler_params=pltpu.CompilerParams(
            dimension_semantics=("parallel", "arbitrary"), vmem_limit_bytes=VMEM_LIMIT_BYTES),
        name="mix_merge",
    )(sinks, rq, rqd, rk, rkd, rv, sg, intra, cdec, gn_g, aq, ak, ak, av, av, caps,
      x2d, ga, gb, w_ret_out, w_attn_out, w_out)


def _merge_kernel(x_ref, a_ref, b_ref, ga_ref, gb_ref, wr_ref, wa_ref, wo_ref, o_ref):
    branch_a = jnp.dot(a_ref[...], wr_ref[...], preferred_element_type=F32)
    branch_b = jnp.dot(b_ref[...], wa_ref[...], preferred_element_type=F32)
    merged = (_sigmoid(ga_ref[...].astype(F32)) * branch_a
              + _sigmoid(gb_ref[...].astype(F32)) * branch_b)
    o_ref[...] = x_ref[...] + jnp.dot(merged.astype(BF16), wo_ref[...], preferred_element_type=F32)


def _merge(x2d, a, b, ga, gb, w_ret_out, w_attn_out, w_out):
    t = x2d.shape[0]
    tm = TOKEN_TILE
    row = lambda i: (i, 0)
    tile = pl.BlockSpec((tm, D_MODEL), row)
    return pl.pallas_call(
        _merge_kernel,
        grid=(t // tm,),
        in_specs=[tile, tile, tile, tile, tile,
                  _resident((RET_V, D_MODEL)), _resident((ATTN_Q, D_MODEL)),
                  _resident((D_MODEL, D_MODEL))],
        out_specs=tile,
        out_shape=jax.ShapeDtypeStruct((t, D_MODEL), F32),
        compiler_params=pltpu.CompilerParams(
            dimension_semantics=("parallel",), vmem_limit_bytes=VMEM_LIMIT_BYTES),
        name="merge",
    )(x2d, a, b, ga, gb, w_ret_out, w_attn_out, w_out)


def _ffn_kernel(x_ref, g2_ref, wg_ref, wu_ref, wd_ref, gf_ref, o_ref, *, final_norm):
    x = x_ref[...]
    h = _rms_norm(x, g2_ref[...]).astype(BF16)
    gate = jnp.dot(h, wg_ref[...], preferred_element_type=F32)
    up = jnp.dot(h, wu_ref[...], preferred_element_type=F32)
    act = ((gate * _sigmoid(gate)) * up).astype(BF16)
    y = x + jnp.dot(act, wd_ref[...], preferred_element_type=F32)
    o_ref[...] = _rms_norm(y, gf_ref[...]) if final_norm else y


def _ffn(x2d, ln2_g, w_gate, w_up, w_down, lnf_g, final_norm):
    t = x2d.shape[0]
    tm = TOKEN_TILE
    row = lambda i: (i, 0)
    tile = pl.BlockSpec((tm, D_MODEL), row)
    return pl.pallas_call(
        functools.partial(_ffn_kernel, final_norm=final_norm),
        grid=(t // tm,),
        in_specs=[tile, _resident((1, D_MODEL)),
                  _resident((D_MODEL, D_FF)), _resident((D_MODEL, D_FF)),
                  _resident((D_FF, D_MODEL)), _resident((1, D_MODEL))],
        out_specs=tile,
        out_shape=jax.ShapeDtypeStruct((t, D_MODEL), F32),
        compiler_params=pltpu.CompilerParams(
            dimension_semantics=("parallel",), vmem_limit_bytes=VMEM_LIMIT_BYTES),
        name="ffn",
    )(x2d, ln2_g, w_gate, w_up, w_down, lnf_g)


def _rotary_tables(seq_len):
    pos = np.arange(seq_len, dtype=np.float64)

    def cos_sin(dim):
        half = dim // 2
        inv_freq = ROPE_THETA ** (-np.arange(half, dtype=np.float64) / half)
        ang = pos[:, None] * inv_freq[None, :]
        cos, sin = np.cos(ang), np.sin(ang)
        reps = LANES // dim
        return (np.tile(np.concatenate([cos, cos], axis=-1), (1, reps)).astype(np.float32),
                np.tile(np.concatenate([-sin, sin], axis=-1), (1, reps)).astype(np.float32))

    return cos_sin(RET_QK_DIM) + cos_sin(ATTN_HEAD_DIM)


def _decay_tables():
    c = RET_CHUNK
    log_gamma = np.log1p(-np.exp2(-5.0 - np.arange(RET_HEADS, dtype=np.float64)))
    idx = np.arange(c, dtype=np.float64)
    rel = idx[:, None] - idx[None, :]
    intra = np.where(rel[None] >= 0,
                     np.exp(log_gamma[:, None, None] * np.maximum(rel, 0.0)[None]), 0.0)
    q_decay = np.exp(log_gamma[:, None] * (idx + 1.0))
    k_decay = np.exp(log_gamma[:, None] * (c - 1.0 - idx))
    chunk_decay = np.exp(log_gamma * c)[:, None, None]

    def per_row(d):
        lanes = np.repeat(d.T, RET_QK_DIM, axis=1)
        return np.tile(lanes, (RET_STEP_CHUNKS, 1)).astype(np.float32)

    return (intra.astype(np.float32),
            np.broadcast_to(chunk_decay, (RET_HEADS, 1, RET_V_DIM)).astype(np.float32),
            per_row(q_decay), per_row(k_decay))


def _score_caps():
    c = ATTN_BLOCK
    qi = np.arange(c)[:, None]
    kj = np.arange(2 * c)[None, :]
    rel = c + qi - kj
    band = (rel >= 0) & (rel < WINDOW)
    later = np.where(band, np.inf, MASK_VALUE)
    first = np.where(band & (kj >= c), np.inf, MASK_VALUE)
    return np.stack([np.tile(first, (2, 1)), np.tile(later, (2, 1))]).astype(np.float32)


def kernel(x, ln1_g, w_in, b_in, ret_norm_g, w_ret_out, attn_sinks, w_attn_out, w_out,
           ln2_g, w_ffn_gate, w_ffn_up, w_ffn_down, lnf_g):
    batch, seq_len, d_model = x.shape
    depth = w_in.shape[0]
    assert d_model == D_MODEL and w_in.shape[2] == D_IN
    assert seq_len % TOKEN_TILE == 0 and seq_len % ATTN_BLOCK == 0

    tables = _rotary_tables(seq_len)
    intra, cdec, qdec_rows, kdec_rows = _decay_tables()
    caps = _score_caps()
    xs = x.reshape(batch * seq_len, d_model)
    for l in range(depth):
        rq, rqd, rk, rkd, rv, sg, aq, ak, av, ga, gb = _inproj(
            xs, ln1_g[l][None], w_in[l].astype(BF16), b_in[l][None], tables,
            (qdec_rows, kdec_rows), seq_len)
        xs = _mix_merge(xs, rq, rqd, rk, rkd, rv, sg, intra, cdec, ret_norm_g[l][None],
                        aq, ak, av, attn_sinks[l], caps, ga, gb, w_ret_out[l].astype(BF16),
                        w_attn_out[l].astype(BF16), w_out[l].astype(BF16), batch, seq_len)
        xs = _ffn(xs, ln2_g[l][None], w_ffn_gate[l].astype(BF16), w_ffn_up[l].astype(BF16),
                  w_ffn_down[l].astype(BF16), lnf_g[None], final_norm=(l == depth - 1))
    return xs.reshape(batch, seq_len, d_model)
```

```python
import functools
import math

import jax
import jax.numpy as jnp
import numpy as np
from jax import lax
from jax.experimental import pallas as pl
from jax.experimental.pallas import tpu as pltpu

F32 = jnp.float32
BF16 = jnp.bfloat16

D_MODEL = 1024
RET_HEADS = 4
RET_QK_DIM = 128
RET_V_DIM = 256
RET_CHUNK = 128
ATTN_Q_HEADS = 16
ATTN_KV_HEADS = 2
ATTN_HEAD_DIM = 64
WINDOW = 128
ATTN_BLOCK = 128
D_FF = 2816
ROPE_THETA = 10000.0
EPS = 1e-6
MASK_VALUE = -1e30

RET_QK = RET_HEADS * RET_QK_DIM
RET_V = RET_HEADS * RET_V_DIM
ATTN_Q = ATTN_Q_HEADS * ATTN_HEAD_DIM
ATTN_KV = ATTN_KV_HEADS * ATTN_HEAD_DIM
OFF_RQ = 0
OFF_RK = OFF_RQ + RET_QK
OFF_RV = OFF_RK + RET_QK
OFF_RG = OFF_RV + RET_V
OFF_AQ = OFF_RG + RET_V
OFF_AK = OFF_AQ + ATTN_Q
OFF_AV = OFF_AK + ATTN_KV
OFF_GA = OFF_AV + ATTN_KV
OFF_GB = OFF_GA + D_MODEL
D_IN = OFF_GB + D_MODEL

LANES = 128
TOKEN_TILE = 512
RET_STEP_CHUNKS = TOKEN_TILE // RET_CHUNK
ATTN_STEP_BLOCKS = TOKEN_TILE // ATTN_BLOCK
INPROJ_ROWS = 256
FFN_TILE = 1024
FFN_ROWS = 256
ATTN_QK_LEAD = 4
LOG2E = math.log2(math.e)
VMEM_LIMIT_BYTES = 56 * 1024 * 1024


def _sigmoid(x):
    return 1.0 / (1.0 + jnp.exp(-x))


def _rms_norm(x, g):
    ms = jnp.mean(x * x, axis=-1, keepdims=True)
    return (x * lax.rsqrt(ms + EPS)) * g


def _resident(shape):
    nd = len(shape)
    return pl.BlockSpec(shape, lambda *_: (0,) * nd, pipeline_mode=pl.Buffered(1))


def _rot_half128(x, cos, sin_signed):
    return x * cos + pltpu.roll(x, 64, 1) * sin_signed


def _rot_half64(x, cos, sin_signed, first_half):
    partner = jnp.where(first_half, pltpu.roll(x, 96, 1), pltpu.roll(x, 32, 1))
    return x * cos + partner * sin_signed


def _inproj_kernel(x_ref, g_ref, w_ref, b_ref, cr_ref, sr_ref, ca_ref, sa_ref, qdec_ref, kdec_ref,
                   rq_ref, rqd_ref, rk_ref, rkd_ref, rv_ref, sg_ref, aq_ref, ak_ref, av_ref,
                   ga_ref, gb_ref):
    lane = lax.broadcasted_iota(jnp.int32, (INPROJ_ROWS, LANES), 1)
    first_half = (lane % ATTN_HEAD_DIM) < (ATTN_HEAD_DIM // 2)
    q_scale = RET_QK_DIM ** -0.5
    a_scale = LOG2E * ATTN_HEAD_DIM ** -0.5

    def emit(r, h):
        def proj(off, width):
            acc = jnp.dot(h, w_ref[:, off:off + width], preferred_element_type=F32)
            return acc + b_ref[:, off:off + width]

        cr, sr = cr_ref[r, :], sr_ref[r, :]
        ca, sa = ca_ref[r, :], sa_ref[r, :]
        rq = proj(OFF_RQ, RET_QK)
        rk = proj(OFF_RK, RET_QK)
        for hd in range(RET_HEADS):
            sl = slice(hd * LANES, (hd + 1) * LANES)
            q = _rot_half128(rq[:, sl], cr, sr) * q_scale
            rq_ref[r, sl] = q.astype(BF16)
            rqd_ref[r, sl] = (q * qdec_ref[r, sl]).astype(BF16)
            k = _rot_half128(rk[:, sl], cr, sr)
            rk_ref[r, sl] = k.astype(BF16)
            rkd_ref[r, sl] = (k * kdec_ref[r, sl]).astype(BF16)
        rv_ref[r, :] = proj(OFF_RV, RET_V).astype(BF16)
        rg = proj(OFF_RG, RET_V)
        sg_ref[r, :] = (rg * _sigmoid(rg)).astype(BF16)
        aq = proj(OFF_AQ, ATTN_Q)
        for c in range(ATTN_Q // LANES):
            sl = slice(c * LANES, (c + 1) * LANES)
            aq_ref[r, sl] = (_rot_half64(aq[:, sl], ca, sa, first_half) * a_scale).astype(BF16)
        akv = proj(OFF_AK, 2 * ATTN_KV)
        ak_ref[r, :] = _rot_half64(akv[:, :ATTN_KV], ca, sa, first_half).astype(BF16)
        av_ref[r, :] = akv[:, ATTN_KV:].astype(BF16)
        ga_ref[r, :] = proj(OFF_GA, D_MODEL).astype(BF16)
        gb_ref[r, :] = proj(OFF_GB, D_MODEL).astype(BF16)

    groups = [slice(i * INPROJ_ROWS, (i + 1) * INPROJ_ROWS)
              for i in range(TOKEN_TILE // INPROJ_ROWS)]
    hs = [_rms_norm(x_ref[r, :], g_ref[...]).astype(BF16) for r in groups]
    for r, h in zip(groups, hs):
        emit(r, h)


def _inproj(x2d, ln_g, w_in, b_in, tables, row_decays, seq_len):
    t = x2d.shape[0]
    tm = TOKEN_TILE
    pos_tiles = seq_len // tm
    row = lambda i: (i, 0)
    pos = lambda i: (i % pos_tiles, 0)
    widths = (RET_QK, RET_QK, RET_QK, RET_QK, RET_V, RET_V, ATTN_Q, ATTN_KV, ATTN_KV,
              D_MODEL, D_MODEL)
    return pl.pallas_call(
        _inproj_kernel,
        grid=(t // tm,),
        in_specs=[
            pl.BlockSpec((tm, D_MODEL), row),
            _resident((1, D_MODEL)),
            _resident((D_MODEL, D_IN)),
            _resident((1, D_IN)),
            pl.BlockSpec((tm, LANES), pos),
            pl.BlockSpec((tm, LANES), pos),
            pl.BlockSpec((tm, LANES), pos),
            pl.BlockSpec((tm, LANES), pos),
            _resident((tm, RET_QK)),
            _resident((tm, RET_QK)),
        ],
        out_specs=[pl.BlockSpec((tm, w), row) for w in widths],
        out_shape=[jax.ShapeDtypeStruct((t, w), BF16) for w in widths],
        compiler_params=pltpu.CompilerParams(
            dimension_semantics=("parallel",), vmem_limit_bytes=VMEM_LIMIT_BYTES),
        name="inproj",
    )(x2d, ln_g, w_in, b_in, *tables, *row_decays)


def _ret_kernel(q_ref, qd_ref, k_ref, kd_ref, v_ref, sg_ref, intra_ref, cdec_ref, gn_ref,
                o_ref, state_ref):
    @pl.when(pl.program_id(1) == 0)
    def _():
        state_ref[...] = jnp.zeros_like(state_ref)

    c = RET_CHUNK
    units = [(hd, ci) for hd in range(RET_HEADS) for ci in range(RET_STEP_CHUNKS)]
    qk_of = lambda hd: slice(hd * RET_QK_DIM, (hd + 1) * RET_QK_DIM)
    vs_of = lambda hd: slice(hd * RET_V_DIM, (hd + 1) * RET_V_DIM)
    rows_of = lambda ci: slice(ci * c, (ci + 1) * c)

    scores, update = {}, {}
    for hd, ci in units:
        rows, qk = rows_of(ci), qk_of(hd)
        scores[hd, ci] = lax.dot_general(q_ref[rows, qk], k_ref[rows, qk],
                                         (((1,), (1,)), ((), ())), preferred_element_type=F32)
        update[hd, ci] = lax.dot_general(kd_ref[rows, qk], v_ref[rows, vs_of(hd)],
                                         (((0,), (0,)), ((), ())), preferred_element_type=F32)

    y = {}
    for hd in range(RET_HEADS):
        state = state_ref[hd]
        for ci in range(RET_STEP_CHUNKS):
            rows = rows_of(ci)
            lhs = jnp.concatenate([(scores[hd, ci] * intra_ref[hd]).astype(BF16),
                                   qd_ref[rows, qk_of(hd)]], axis=1)
            rhs = jnp.concatenate([v_ref[rows, vs_of(hd)], state.astype(BF16)], axis=0)
            y[hd, ci] = jnp.dot(lhs, rhs, preferred_element_type=F32)
            state = state * cdec_ref[hd] + update[hd, ci]
        state_ref[hd] = state

    for hd, ci in units:
        rows, vs = rows_of(ci), vs_of(hd)
        yc = y[hd, ci] - jnp.mean(y[hd, ci], axis=-1, keepdims=True)
        var = jnp.mean(yc * yc, axis=-1, keepdims=True)
        yn = (yc * lax.rsqrt(var + EPS)) * gn_ref[:, vs]
        o_ref[rows, vs] = (sg_ref[rows, vs].astype(F32) * yn).astype(BF16)


def _retention(rq, rqd, rk, rkd, rv, sg, intra, cdec, gn_g, batch, seq_len):
    t = rq.shape[0]
    rows = RET_STEP_CHUNKS * RET_CHUNK
    n = seq_len // rows
    row = lambda b, i: (b * n + i, 0)
    qk_tile = pl.BlockSpec((rows, RET_QK), row)
    v_tile = pl.BlockSpec((rows, RET_V), row)
    return pl.pallas_call(
        _ret_kernel,
        grid=(batch, n),
        in_specs=[
            qk_tile, qk_tile, qk_tile, qk_tile, v_tile, v_tile,
            _resident((RET_HEADS, RET_CHUNK, RET_CHUNK)),
            _resident((RET_HEADS, 1, RET_V_DIM)),
            _resident((1, RET_V)),
        ],
        out_specs=v_tile,
        out_shape=jax.ShapeDtypeStruct((t, RET_V), BF16),
        scratch_shapes=[pltpu.VMEM((RET_HEADS, RET_QK_DIM, RET_V_DIM), F32)],
        compiler_params=pltpu.CompilerParams(
            dimension_semantics=("parallel", "arbitrary"), vmem_limit_bytes=VMEM_LIMIT_BYTES),
        name="retention",
    )(rq, rqd, rk, rkd, rv, sg, intra, cdec, gn_g)


def _attn_kernel(sink_ref, q_ref, kp_ref, kc_ref, vp_ref, vc_ref, cap_ref, o_ref):
    c = ATTN_BLOCK
    kv_rows = (ATTN_STEP_BLOCKS + 1) * c
    first = pl.program_id(1) == 0
    low_kv = lax.broadcasted_iota(jnp.int32, (kv_rows, LANES), 1) < ATTN_HEAD_DIM
    low_q = lax.broadcasted_iota(jnp.int32, (c, LANES), 1) < ATTN_HEAD_DIM
    eye = (lax.broadcasted_iota(jnp.int32, (c, c), 0) == lax.broadcasted_iota(jnp.int32, (c, c), 1))
    eye2 = jnp.concatenate([eye, eye], axis=0)

    keys = jnp.concatenate([kp_ref[...], kc_ref[...]], axis=0).astype(F32)
    vals = jnp.concatenate([vp_ref[...], vc_ref[...]], axis=0).astype(F32)
    keys_sw = pltpu.roll(keys, ATTN_HEAD_DIM, 1)
    vals_sw = pltpu.roll(vals, ATTN_HEAD_DIM, 1)
    pairs_per_kv = ATTN_Q_HEADS // ATTN_KV_HEADS // 2
    neg_inf = jnp.float32(-jnp.inf)
    kk_all = [jnp.where(low_kv, keys, keys_sw).astype(BF16),
              jnp.where(low_kv, keys_sw, keys).astype(BF16)]
    vv_all = [jnp.where(low_kv, vals, vals_sw).astype(BF16),
              jnp.where(low_kv, vals_sw, vals).astype(BF16)]

    def qk_scores(p, blk):
        g = p // pairs_per_kv
        qp = q_ref[blk * c:(blk + 1) * c, p * LANES:(p + 1) * LANES]
        zero = jnp.zeros_like(qp)
        q2 = jnp.concatenate([jnp.where(low_q, qp, zero), jnp.where(low_q, zero, qp)], axis=0)
        return lax.dot_general(q2, kk_all[g][blk * c:(blk + 2) * c], (((1,), (1,)), ((), ())),
                               preferred_element_type=F32)

    def softmax_pv(p, blk, s):
        g = p // pairs_per_kv
        fill = jnp.concatenate(
            [jnp.where(eye, sink_ref[2 * p] * LOG2E, neg_inf),
             jnp.where(eye, sink_ref[2 * p + 1] * LOG2E, neg_inf)], axis=0)
        cap = cap_ref[jnp.where(first, 0, 1)] if blk == 0 else cap_ref[1]
        sp = jnp.maximum(jnp.minimum(s[:, :c], cap[:, :c]), fill)
        sc = jnp.minimum(s[:, c:], cap[:, c:])
        m = jnp.max(jnp.maximum(sp, sc), axis=-1, keepdims=True)
        ep = jnp.exp2(sp - m)
        ec = jnp.exp2(sc - m)
        denom = jnp.sum(ep + ec, axis=-1, keepdims=True)
        e = jnp.concatenate([jnp.where(eye2, 0.0, ep), ec], axis=1).astype(BF16)
        o2 = jnp.dot(e, vv_all[g][blk * c:(blk + 2) * c], preferred_element_type=F32)
        o2 = o2 * (1.0 / denom)
        o_ref[blk * c:(blk + 1) * c, p * LANES:(p + 1) * LANES] = (
            jnp.where(low_q, o2[:c], o2[c:]).astype(BF16))

    units = [(p, blk) for p in range(ATTN_Q_HEADS // 2) for blk in range(ATTN_STEP_BLOCKS)]
    pending = {}
    for i in range(len(units) + ATTN_QK_LEAD):
        if i < len(units):
            pending[i] = qk_scores(*units[i])
        if i >= ATTN_QK_LEAD:
            j = i - ATTN_QK_LEAD
            softmax_pv(*units[j], pending.pop(j))


def _attention(aq, ak, av, sinks, caps, batch, seq_len):
    t = aq.shape[0]
    c = ATTN_BLOCK
    rows = ATTN_STEP_BLOCKS * c
    n = seq_len // rows
    cur = lambda b, i: (b * n + i, 0)
    prev = lambda b, i: ((b * n + i) * ATTN_STEP_BLOCKS - jnp.where(i == 0, 0, 1), 0)
    return pl.pallas_call(
        _attn_kernel,
        grid=(batch, n),
        in_specs=[
            pl.BlockSpec(memory_space=pltpu.SMEM),
            pl.BlockSpec((rows, ATTN_Q), cur),
            pl.BlockSpec((c, ATTN_KV), prev),
            pl.BlockSpec((rows, ATTN_KV), cur),
            pl.BlockSpec((c, ATTN_KV), prev),
            pl.BlockSpec((rows, ATTN_KV), cur),
            _resident((2, 2 * c, 2 * c)),
        ],
        out_specs=pl.BlockSpec((rows, ATTN_Q), cur),
        out_shape=jax.ShapeDtypeStruct((t, ATTN_Q), BF16),
        compiler_params=pltpu.CompilerParams(
            dimension_semantics=("parallel", "arbitrary"), vmem_limit_bytes=VMEM_LIMIT_BYTES),
        name="swattn",
    )(sinks, aq, ak, ak, av, av, caps)


def _mixers_kernel(sink_ref, rq_ref, rqd_ref, rk_ref, rkd_ref, rv_ref, sg_ref, intra_ref, cdec_ref,
                   gn_ref, aq_ref, kp_ref, kc_ref, vp_ref, vc_ref, cap_ref,
                   a_ref, b_ref, state_ref):
    _ret_kernel(rq_ref, rqd_ref, rk_ref, rkd_ref, rv_ref, sg_ref, intra_ref, cdec_ref, gn_ref,
                a_ref, state_ref)
    _attn_kernel(sink_ref, aq_ref, kp_ref, kc_ref, vp_ref, vc_ref, cap_ref, b_ref)


def _mix_merge_kernel(sink_ref, rq_ref, rqd_ref, rk_ref, rkd_ref, rv_ref, sg_ref, intra_ref,
                      cdec_ref, gn_ref, aq_ref, kp_ref, kc_ref, vp_ref, vc_ref, cap_ref,
                      x_ref, ga_ref, gb_ref, wr_ref, wa_ref, wo_ref,
                      o_ref, state_ref, a_ref, b_ref):
    _mixers_kernel(sink_ref, rq_ref, rqd_ref, rk_ref, rkd_ref, rv_ref, sg_ref, intra_ref, cdec_ref,
                   gn_ref, aq_ref, kp_ref, kc_ref, vp_ref, vc_ref, cap_ref, a_ref, b_ref, state_ref)
    _merge_kernel(x_ref, a_ref, b_ref, ga_ref, gb_ref, wr_ref, wa_ref, wo_ref, o_ref)


def _mix_merge(x2d, rq, rqd, rk, rkd, rv, sg, intra, cdec, gn_g, aq, ak, av, sinks, caps, ga, gb,
               w_ret_out, w_attn_out, w_out, batch, seq_len):
    assert RET_STEP_CHUNKS * RET_CHUNK == ATTN_STEP_BLOCKS * ATTN_BLOCK == TOKEN_TILE
    t = rq.shape[0]
    c = ATTN_BLOCK
    rows = TOKEN_TILE
    n = seq_len // rows
    cur = lambda b, i: (b * n + i, 0)
    prev = lambda b, i: ((b * n + i) * ATTN_STEP_BLOCKS - jnp.where(i == 0, 0, 1), 0)
    qk_tile = pl.BlockSpec((rows, RET_QK), cur)
    wide_tile = pl.BlockSpec((rows, D_MODEL), cur)
    return pl.pallas_call(
        _mix_merge_kernel,
        grid=(batch, n),
        in_specs=[
            pl.BlockSpec(memory_space=pltpu.SMEM),
            qk_tile, qk_tile, qk_tile, qk_tile, wide_tile, wide_tile,
            _resident((RET_HEADS, RET_CHUNK, RET_CHUNK)),
            _resident((RET_HEADS, 1, RET_V_DIM)),
            _resident((1, RET_V)),
            wide_tile,
            pl.BlockSpec((c, ATTN_KV), prev),
            pl.BlockSpec((rows, ATTN_KV), cur),
            pl.BlockSpec((c, ATTN_KV), prev),
            pl.BlockSpec((rows, ATTN_KV), cur),
            _resident((2, 2 * c, 2 * c)),
            wide_tile, wide_tile, wide_tile,
            _resident((RET_V, D_MODEL)), _resident((ATTN_Q, D_MODEL)),
            _resident((D_MODEL, D_MODEL)),
        ],
        out_specs=wide_tile,
        out_shape=jax.ShapeDtypeStruct((t, D_MODEL), F32),
        scratch_shapes=[pltpu.VMEM((RET_HEADS, RET_QK_DIM, RET_V_DIM), F32),
                        pltpu.VMEM((rows, RET_V), BF16), pltpu.VMEM((rows, ATTN_Q), BF16)],
        compiler_params=pltpu.CompilerParams(
            dimension_semantics=("parallel", "arbitrary"), vmem_limit_bytes=VMEM_LIMIT_BYTES),
        name="mix_merge",
    )(sinks, rq, rqd, rk, rkd, rv, sg, intra, cdec, gn_g, aq, ak, ak, av, av, caps,
      x2d, ga, gb, w_ret_out, w_attn_out, w_out)


def _merge_kernel(x_ref, a_ref, b_ref, ga_ref, gb_ref, wr_ref, wa_ref, wo_ref, o_ref):
    branch_a = jnp.dot(a_ref[...], wr_ref[...], preferred_element_type=F32)
    branch_b = jnp.dot(b_ref[...], wa_ref[...], preferred_element_type=F32)
    merged = (_sigmoid(ga_ref[...].astype(F32)) * branch_a
              + _sigmoid(gb_ref[...].astype(F32)) * branch_b)
    o_ref[...] = x_ref[...] + jnp.dot(merged.astype(BF16), wo_ref[...], preferred_element_type=F32)


def _merge(x2d, a, b, ga, gb, w_ret_out, w_attn_out, w_out):
    t = x2d.shape[0]
    tm = TOKEN_TILE
    row = lambda i: (i, 0)
    tile = pl.BlockSpec((tm, D_MODEL), row)
    return pl.pallas_call(
        _merge_kernel,
        grid=(t // tm,),
        in_specs=[tile, tile, tile, tile, tile,
                  _resident((RET_V, D_MODEL)), _resident((ATTN_Q, D_MODEL)),
                  _resident((D_MODEL, D_MODEL))],
        out_specs=tile,
        out_shape=jax.ShapeDtypeStruct((t, D_MODEL), F32),
        compiler_params=pltpu.CompilerParams(
            dimension_semantics=("parallel",), vmem_limit_bytes=VMEM_LIMIT_BYTES),
        name="merge",
    )(x2d, a, b, ga, gb, w_ret_out, w_attn_out, w_out)


def _ffn_kernel(x_ref, g2_ref, wg_ref, wu_ref, wd_ref, gf_ref, o_ref, *, final_norm):
    groups = [slice(i * FFN_ROWS, (i + 1) * FFN_ROWS) for i in range(FFN_TILE // FFN_ROWS)]

    def gate_up(r):
        h = _rms_norm(x_ref[r, :], g2_ref[...]).astype(BF16)
        return (jnp.dot(h, wg_ref[...], preferred_element_type=F32),
                jnp.dot(h, wu_ref[...], preferred_element_type=F32))

    def finish(r, gate, up):
        act = ((gate * _sigmoid(gate)) * up).astype(BF16)
        y = x_ref[r, :] + jnp.dot(act, wd_ref[...], preferred_element_type=F32)
        o_ref[r, :] = _rms_norm(y, gf_ref[...]) if final_norm else y

    pending = {}
    for i in range(len(groups) + 1):
        if i < len(groups):
            pending[i] = gate_up(groups[i])
        if i >= 1:
            finish(groups[i - 1], *pending.pop(i - 1))


def _ffn(x2d, ln2_g, w_gate, w_up, w_down, lnf_g, final_norm):
    t = x2d.shape[0]
    tm = FFN_TILE
    assert t % tm == 0
    row = lambda i: (i, 0)
    tile = pl.BlockSpec((tm, D_MODEL), row)
    return pl.pallas_call(
        functools.partial(_ffn_kernel, final_norm=final_norm),
        grid=(t // tm,),
        in_specs=[tile, _resident((1, D_MODEL)),
                  _resident((D_MODEL, D_FF)), _resident((D_MODEL, D_FF)),
                  _resident((D_FF, D_MODEL)), _resident((1, D_MODEL))],
        out_specs=tile,
        out_shape=jax.ShapeDtypeStruct((t, D_MODEL), F32),
        compiler_params=pltpu.CompilerParams(
            dimension_semantics=("parallel",), vmem_limit_bytes=VMEM_LIMIT_BYTES),
        name="ffn",
    )(x2d, ln2_g, w_gate, w_up, w_down, lnf_g)


def _rotary_tables(seq_len):
    pos = np.arange(seq_len, dtype=np.float64)

    def cos_sin(dim):
        half = dim // 2
        inv_freq = ROPE_THETA ** (-np.arange(half, dtype=np.float64) / half)
        ang = pos[:, None] * inv_freq[None, :]
        cos, sin = np.cos(ang), np.sin(ang)
        reps = LANES // dim
        return (np.tile(np.concatenate([cos, cos], axis=-1), (1, reps)).astype(np.float32),
                np.tile(np.concatenate([-sin, sin], axis=-1), (1, reps)).astype(np.float32))

    return cos_sin(RET_QK_DIM) + cos_sin(ATTN_HEAD_DIM)


def _decay_tables():
    c = RET_CHUNK
    log_gamma = np.log1p(-np.exp2(-5.0 - np.arange(RET_HEADS, dtype=np.float64)))
    idx = np.arange(c, dtype=np.float64)
    rel = idx[:, None] - idx[None, :]
    intra = np.where(rel[None] >= 0,
                     np.exp(log_gamma[:, None, None] * np.maximum(rel, 0.0)[None]), 0.0)
    q_decay = np.exp(log_gamma[:, None] * (idx + 1.0))
    k_decay = np.exp(log_gamma[:, None] * (c - 1.0 - idx))
    chunk_decay = np.exp(log_gamma * c)[:, None, None]

    def per_row(d):
        lanes = np.repeat(d.T, RET_QK_DIM, axis=1)
        return np.tile(lanes, (RET_STEP_CHUNKS, 1)).astype(np.float32)

    return (intra.astype(np.float32),
            np.broadcast_to(chunk_decay, (RET_HEADS, 1, RET_V_DIM)).astype(np.float32),
            per_row(q_decay), per_row(k_decay))


def _score_caps():
    c = ATTN_BLOCK
    qi = np.arange(c)[:, None]
    kj = np.arange(2 * c)[None, :]
    rel = c + qi - kj
    band = (rel >= 0) & (rel < WINDOW)
    later = np.where(band, np.inf, MASK_VALUE)
    first = np.where(band & (kj >= c), np.inf, MASK_VALUE)
    return np.stack([np.tile(first, (2, 1)), np.tile(later, (2, 1))]).astype(np.float32)


def kernel(x, ln1_g, w_in, b_in, ret_norm_g, w_ret_out, attn_sinks, w_attn_out, w_out,
           ln2_g, w_ffn_gate, w_ffn_up, w_ffn_down, lnf_g):
    batch, seq_len, d_model = x.shape
    depth = w_in.shape[0]
    assert d_model == D_MODEL and w_in.shape[2] == D_IN
    assert seq_len % TOKEN_TILE == 0 and seq_len % ATTN_BLOCK == 0

    tables = _rotary_tables(seq_len)
    intra, cdec, qdec_rows, kdec_rows = _decay_tables()
    caps = _score_caps()
    xs = x.reshape(batch * seq_len, d_model)
    for l in range(depth):
        rq, rqd, rk, rkd, rv, sg, aq, ak, av, ga, gb = _inproj(
            xs, ln1_g[l][None], w_in[l].astype(BF16), b_in[l][None], tables,
            (qdec_rows, kdec_rows), seq_len)
        xs = _mix_merge(xs, rq, rqd, rk, rkd, rv, sg, intra, cdec, ret_norm_g[l][None],
                        aq, ak, av, attn_sinks[l], caps, ga, gb, w_ret_out[l].astype(BF16),
                        w_attn_out[l].astype(BF16), w_out[l].astype(BF16), batch, seq_len)
        xs = _ffn(xs, ln2_g[l][None], w_ffn_gate[l].astype(BF16), w_ffn_up[l].astype(BF16),
                  w_ffn_down[l].astype(BF16), lnf_g[None], final_norm=(l == depth - 1))
    return xs.reshape(batch, seq_len, d_model)
```

```python
import functools
import math

import jax
import jax.numpy as jnp
import numpy as np
from jax import lax
from jax.experimental import pallas as pl
from jax.experimental.pallas import tpu as pltpu

F32 = jnp.float32
BF16 = jnp.bfloat16

D_MODEL = 1024
RET_HEADS = 4
RET_QK_DIM = 128
RET_V_DIM = 256
RET_CHUNK = 128
ATTN_Q_HEADS = 16
ATTN_KV_HEADS = 2
ATTN_HEAD_DIM = 64
WINDOW = 128
ATTN_BLOCK = 128
D_FF = 2816
ROPE_THETA = 10000.0
EPS = 1e-6
MASK_VALUE = -1e30

RET_QK = RET_HEADS * RET_QK_DIM
RET_V = RET_HEADS * RET_V_DIM
ATTN_Q = ATTN_Q_HEADS * ATTN_HEAD_DIM
ATTN_KV = ATTN_KV_HEADS * ATTN_HEAD_DIM
OFF_RQ = 0
OFF_RK = OFF_RQ + RET_QK
OFF_RV = OFF_RK + RET_QK
OFF_RG = OFF_RV + RET_V
OFF_AQ = OFF_RG + RET_V
OFF_AK = OFF_AQ + ATTN_Q
OFF_AV = OFF_AK + ATTN_KV
OFF_GA = OFF_AV + ATTN_KV
OFF_GB = OFF_GA + D_MODEL
D_IN = OFF_GB + D_MODEL

PACKED = {}
for _name, _width in (("rq", RET_QK), ("rq_dec", RET_QK), ("rk", RET_QK), ("rk_dec", RET_QK),
                      ("rv", RET_V), ("swish_gate", RET_V), ("aq", ATTN_Q), ("gate_a", D_MODEL),
                      ("gate_b", D_MODEL), ("ak", ATTN_KV), ("av", ATTN_KV)):
    PACKED[_name] = (sum(w for _, w in PACKED.values()), _width)
PACKED_WIDTH = sum(w for _, w in PACKED.values())
assert PACKED["ak"][0] % (2 * ATTN_KV) == 0 and PACKED["av"][0] == PACKED["ak"][0] + ATTN_KV

LANES = 128
TOKEN_TILE = 512
RET_STEP_CHUNKS = TOKEN_TILE // RET_CHUNK
ATTN_STEP_BLOCKS = TOKEN_TILE // ATTN_BLOCK
MERGE_ROWS = 256
INPROJ_ROWS = 256
FFN_TILE = 1024
FFN_ROWS = 256
ATTN_QK_LEAD = 4
LOG2E = math.log2(math.e)
VMEM_LIMIT_BYTES = 56 * 1024 * 1024


def _sigmoid(x):
    return 1.0 / (1.0 + jnp.exp(-x))


def _rms_norm(x, g):
    ms = jnp.mean(x * x, axis=-1, keepdims=True)
    return (x * lax.rsqrt(ms + EPS)) * g


def _resident(shape):
    nd = len(shape)
    return pl.BlockSpec(shape, lambda *_: (0,) * nd, pipeline_mode=pl.Buffered(1))


def _rot_half128(x, cos, sin_signed):
    return x * cos + pltpu.roll(x, 64, 1) * sin_signed


def _rot_half64(x, cos, sin_signed, first_half):
    partner = jnp.where(first_half, pltpu.roll(x, 96, 1), pltpu.roll(x, 32, 1))
    return x * cos + partner * sin_signed


def _packed_views(p_ref):
    return [p_ref.at[:, off:off + width] for off, width in PACKED.values()]


def _inproj_kernel(x_ref, g_ref, w_ref, b_ref, rot_ref, qdec_ref, kdec_ref, p_ref):
    (rq_ref, rqd_ref, rk_ref, rkd_ref, rv_ref, sg_ref, aq_ref, ga_ref, gb_ref,
     ak_ref, av_ref) = _packed_views(p_ref)
    cr_ref, sr_ref, ca_ref, sa_ref = [rot_ref.at[:, i * LANES:(i + 1) * LANES] for i in range(4)]
    lane = lax.broadcasted_iota(jnp.int32, (INPROJ_ROWS, LANES), 1)
    first_half = (lane % ATTN_HEAD_DIM) < (ATTN_HEAD_DIM // 2)
    q_scale = RET_QK_DIM ** -0.5
    a_scale = LOG2E * ATTN_HEAD_DIM ** -0.5

    def emit(r, h):
        def proj(off, width):
            acc = jnp.dot(h, w_ref[:, off:off + width], preferred_element_type=F32)
            return acc + b_ref[:, off:off + width]

        cr, sr = cr_ref[r, :], sr_ref[r, :]
        ca, sa = ca_ref[r, :], sa_ref[r, :]
        rq = proj(OFF_RQ, RET_QK)
        rk = proj(OFF_RK, RET_QK)
        for hd in range(RET_HEADS):
            sl = slice(hd * LANES, (hd + 1) * LANES)
            q = _rot_half128(rq[:, sl], cr, sr) * q_scale
            rq_ref[r, sl] = q.astype(BF16)
            rqd_ref[r, sl] = (q * qdec_ref[r, sl]).astype(BF16)
            k = _rot_half128(rk[:, sl], cr, sr)
            rk_ref[r, sl] = k.astype(BF16)
            rkd_ref[r, sl] = (k * kdec_ref[r, sl]).astype(BF16)
        rv_ref[r, :] = proj(OFF_RV, RET_V).astype(BF16)
        rg = proj(OFF_RG, RET_V)
        sg_ref[r, :] = (rg * _sigmoid(rg)).astype(BF16)
        aq = proj(OFF_AQ, ATTN_Q)
        for c in range(ATTN_Q // LANES):
            sl = slice(c * LANES, (c + 1) * LANES)
            aq_ref[r, sl] = (_rot_half64(aq[:, sl], ca, sa, first_half) * a_scale).astype(BF16)
        akv = proj(OFF_AK, 2 * ATTN_KV)
        ak_ref[r, :] = _rot_half64(akv[:, :ATTN_KV], ca, sa, first_half).astype(BF16)
        av_ref[r, :] = akv[:, ATTN_KV:].astype(BF16)
        ga_ref[r, :] = proj(OFF_GA, D_MODEL).astype(BF16)
        gb_ref[r, :] = proj(OFF_GB, D_MODEL).astype(BF16)

    groups = [slice(i * INPROJ_ROWS, (i + 1) * INPROJ_ROWS)
              for i in range(TOKEN_TILE // INPROJ_ROWS)]
    hs = [_rms_norm(x_ref[r, :], g_ref[...]).astype(BF16) for r in groups]
    for r, h in zip(groups, hs):
        emit(r, h)


def _inproj(x2d, ln_g, w_in, b_in, tables, row_decays, seq_len):
    t = x2d.shape[0]
    tm = TOKEN_TILE
    pos_tiles = seq_len // tm
    row = lambda i: (i, 0)
    pos = lambda i: (i % pos_tiles, 0)
    return pl.pallas_call(
        _inproj_kernel,
        grid=(t // tm,),
        in_specs=[
            pl.BlockSpec((tm, D_MODEL), row),
            _resident((1, D_MODEL)),
            _resident((D_MODEL, D_IN)),
            _resident((1, D_IN)),
            pl.BlockSpec((tm, 4 * LANES), pos),
            _resident((tm, RET_QK)),
            _resident((tm, RET_QK)),
        ],
        out_specs=pl.BlockSpec((tm, PACKED_WIDTH), row),
        out_shape=jax.ShapeDtypeStruct((t, PACKED_WIDTH), BF16),
        compiler_params=pltpu.CompilerParams(
            dimension_semantics=("parallel",), vmem_limit_bytes=VMEM_LIMIT_BYTES),
        name="inproj",
    )(x2d, ln_g, w_in, b_in, tables, *row_decays)


def _retention(q_ref, qd_ref, k_ref, kd_ref, v_ref, sg_ref, intra_ref, cdec_ref, gn_ref,
               o_ref, state_ref):
    c = RET_CHUNK
    units = [(hd, ci) for hd in range(RET_HEADS) for ci in range(RET_STEP_CHUNKS)]
    qk_of = lambda hd: slice(hd * RET_QK_DIM, (hd + 1) * RET_QK_DIM)
    vs_of = lambda hd: slice(hd * RET_V_DIM, (hd + 1) * RET_V_DIM)
    rows_of = lambda ci: slice(ci * c, (ci + 1) * c)

    scores, update = {}, {}
    for hd, ci in units:
        rows, qk = rows_of(ci), qk_of(hd)
        scores[hd, ci] = lax.dot_general(q_ref[rows, qk], k_ref[rows, qk],
                                         (((1,), (1,)), ((), ())), preferred_element_type=F32)
        update[hd, ci] = lax.dot_general(kd_ref[rows, qk], v_ref[rows, vs_of(hd)],
                                         (((0,), (0,)), ((), ())), preferred_element_type=F32)

    y = {}
    for hd in range(RET_HEADS):
        state = state_ref[hd]
        for ci in range(RET_STEP_CHUNKS):
            rows = rows_of(ci)
            lhs = jnp.concatenate([(scores[hd, ci] * intra_ref[hd]).astype(BF16),
                                   qd_ref[rows, qk_of(hd)]], axis=1)
            rhs = jnp.concatenate([v_ref[rows, vs_of(hd)], state.astype(BF16)], axis=0)
            y[hd, ci] = jnp.dot(lhs, rhs, preferred_element_type=F32)
            state = state * cdec_ref[hd] + update[hd, ci]
        state_ref[hd] = state

    for hd, ci in units:
        rows, vs = rows_of(ci), vs_of(hd)
        yc = y[hd, ci] - jnp.mean(y[hd, ci], axis=-1, keepdims=True)
        var = jnp.mean(yc * yc, axis=-1, keepdims=True)
        yn = (yc * lax.rsqrt(var + EPS)) * gn_ref[:, vs]
        o_ref[rows, vs] = (sg_ref[rows, vs].astype(F32) * yn).astype(BF16)


def _attention(sink_ref, q_ref, kp_ref, kc_ref, vp_ref, vc_ref, cap_ref, o_ref):
    c = ATTN_BLOCK
    kv_rows = (ATTN_STEP_BLOCKS + 1) * c
    first = pl.program_id(1) == 0
    low_kv = lax.broadcasted_iota(jnp.int32, (kv_rows, LANES), 1) < ATTN_HEAD_DIM
    low_q = lax.broadcasted_iota(jnp.int32, (c, LANES), 1) < ATTN_HEAD_DIM
    eye = (lax.broadcasted_iota(jnp.int32, (c, c), 0) == lax.broadcasted_iota(jnp.int32, (c, c), 1))
    eye2 = jnp.concatenate([eye, eye], axis=0)

    keys = jnp.concatenate([kp_ref[...], kc_ref[...]], axis=0).astype(F32)
    vals = jnp.concatenate([vp_ref[...], vc_ref[...]], axis=0).astype(F32)
    keys_sw = pltpu.roll(keys, ATTN_HEAD_DIM, 1)
    vals_sw = pltpu.roll(vals, ATTN_HEAD_DIM, 1)
    pairs_per_kv = ATTN_Q_HEADS // ATTN_KV_HEADS // 2
    neg_inf = jnp.float32(-jnp.inf)
    kk_all = [jnp.where(low_kv, keys, keys_sw).astype(BF16),
              jnp.where(low_kv, keys_sw, keys).astype(BF16)]
    vv_all = [jnp.where(low_kv, vals, vals_sw).astype(BF16),
              jnp.where(low_kv, vals_sw, vals).astype(BF16)]

    def qk_scores(p, blk):
        g = p // pairs_per_kv
        qp = q_ref[blk * c:(blk + 1) * c, p * LANES:(p + 1) * LANES]
        zero = jnp.zeros_like(qp)
        q2 = jnp.concatenate([jnp.where(low_q, qp, zero), jnp.where(low_q, zero, qp)], axis=0)
        return lax.dot_general(q2, kk_all[g][blk * c:(blk + 2) * c], (((1,), (1,)), ((), ())),
                               preferred_element_type=F32)

    def softmax_pv(p, blk, s):
        g = p // pairs_per_kv
        fill = jnp.concatenate(
            [jnp.where(eye, sink_ref[2 * p] * LOG2E, neg_inf),
             jnp.where(eye, sink_ref[2 * p + 1] * LOG2E, neg_inf)], axis=0)
        cap = cap_ref[jnp.where(first, 0, 1)] if blk == 0 else cap_ref[1]
        sp = jnp.maximum(jnp.minimum(s[:, :c], cap[:, :c]), fill)
        sc = jnp.minimum(s[:, c:], cap[:, c:])
        m = jnp.max(jnp.maximum(sp, sc), axis=-1, keepdims=True)
        ep = jnp.exp2(sp - m)
        ec = jnp.exp2(sc - m)
        denom = jnp.sum(ep + ec, axis=-1, keepdims=True)
        e = jnp.concatenate([jnp.where(eye2, 0.0, ep), ec], axis=1).astype(BF16)
        o2 = jnp.dot(e, vv_all[g][blk * c:(blk + 2) * c], preferred_element_type=F32)
        o2 = o2 * (1.0 / denom)
        o_ref[blk * c:(blk + 1) * c, p * LANES:(p + 1) * LANES] = (
            jnp.where(low_q, o2[:c], o2[c:]).astype(BF16))

    units = [(p, blk) for p in range(ATTN_Q_HEADS // 2) for blk in range(ATTN_STEP_BLOCKS)]
    pending = {}
    for i in range(len(units) + ATTN_QK_LEAD):
        if i < len(units):
            pending[i] = qk_scores(*units[i])
        if i >= ATTN_QK_LEAD:
            j = i - ATTN_QK_LEAD
            softmax_pv(*units[j], pending.pop(j))


def _mix_merge_kernel(sink_ref, p_ref, kv_prev_ref, x_ref, intra_ref, cdec_ref, gn_ref, cap_ref,
                      wr_ref, wa_ref, wo_ref, o_ref, state_ref, a_ref, b_ref):
    (rq_ref, rqd_ref, rk_ref, rkd_ref, rv_ref, sg_ref, aq_ref, ga_ref, gb_ref,
     kc_ref, vc_ref) = _packed_views(p_ref)
    kp_ref = kv_prev_ref.at[:, :ATTN_KV]
    vp_ref = kv_prev_ref.at[:, ATTN_KV:]

    @pl.when(pl.program_id(1) == 0)
    def _():
        state_ref[...] = jnp.zeros_like(state_ref)

    _retention(rq_ref, rqd_ref, rk_ref, rkd_ref, rv_ref, sg_ref, intra_ref, cdec_ref, gn_ref,
               a_ref, state_ref)
    _attention(sink_ref, aq_ref, kp_ref, kc_ref, vp_ref, vc_ref, cap_ref, b_ref)
    _merge(x_ref, a_ref, b_ref, ga_ref, gb_ref, wr_ref, wa_ref, wo_ref, o_ref)


def _mix_merge(x2d, packed, intra, cdec, gn_g, sinks, caps, w_ret_out, w_attn_out, w_out,
               batch, seq_len):
    assert RET_STEP_CHUNKS * RET_CHUNK == ATTN_STEP_BLOCKS * ATTN_BLOCK == TOKEN_TILE
    t = x2d.shape[0]
    c = ATTN_BLOCK
    rows = TOKEN_TILE
    n = seq_len // rows
    cur = lambda b, i: (b * n + i, 0)
    kv_prev = lambda b, i: ((b * n + i) * ATTN_STEP_BLOCKS - jnp.where(i == 0, 0, 1),
                            PACKED["ak"][0] // (2 * ATTN_KV))
    return pl.pallas_call(
        _mix_merge_kernel,
        grid=(batch, n),
        in_specs=[
            pl.BlockSpec(memory_space=pltpu.SMEM),
            pl.BlockSpec((rows, PACKED_WIDTH), cur),
            pl.BlockSpec((c, 2 * ATTN_KV), kv_prev),
            pl.BlockSpec((rows, D_MODEL), cur),
            _resident((RET_HEADS, RET_CHUNK, RET_CHUNK)),
            _resident((RET_HEADS, 1, RET_V_DIM)),
            _resident((1, RET_V)),
            _resident((2, 2 * c, 2 * c)),
            _resident((RET_V, D_MODEL)), _resident((ATTN_Q, D_MODEL)),
            _resident((D_MODEL, D_MODEL)),
        ],
        out_specs=pl.BlockSpec((rows, D_MODEL), cur),
        out_shape=jax.ShapeDtypeStruct((t, D_MODEL), F32),
        scratch_shapes=[pltpu.VMEM((RET_HEADS, RET_QK_DIM, RET_V_DIM), F32),
                        pltpu.VMEM((rows, RET_V), BF16), pltpu.VMEM((rows, ATTN_Q), BF16)],
        compiler_params=pltpu.CompilerParams(
            dimension_semantics=("parallel", "arbitrary"), vmem_limit_bytes=VMEM_LIMIT_BYTES),
        name="mix_merge",
    )(sinks, packed, packed, x2d, intra, cdec, gn_g, caps, w_ret_out, w_attn_out, w_out)


def _merge(x_ref, a_ref, b_ref, ga_ref, gb_ref, wr_ref, wa_ref, wo_ref, o_ref):
    groups = [slice(i * MERGE_ROWS, (i + 1) * MERGE_ROWS) for i in range(TOKEN_TILE // MERGE_ROWS)]

    def branches(r):
        return (jnp.dot(a_ref[r, :], wr_ref[...], preferred_element_type=F32),
                jnp.dot(b_ref[r, :], wa_ref[...], preferred_element_type=F32))

    def finish(r, branch_a, branch_b):
        merged = (_sigmoid(ga_ref[r, :].astype(F32)) * branch_a
                  + _sigmoid(gb_ref[r, :].astype(F32)) * branch_b)
        o_ref[r, :] = x_ref[r, :] + jnp.dot(merged.astype(BF16), wo_ref[...],
                                            preferred_element_type=F32)

    pending = {}
    for i in range(len(groups) + 1):
        if i < len(groups):
            pending[i] = branches(groups[i])
        if i >= 1:
            finish(groups[i - 1], *pending.pop(i - 1))


def _ffn_kernel(x_ref, g2_ref, wg_ref, wu_ref, wd_ref, gf_ref, o_ref, *, final_norm):
    groups = [slice(i * FFN_ROWS, (i + 1) * FFN_ROWS) for i in range(FFN_TILE // FFN_ROWS)]

    def gate_up(r):
        h = _rms_norm(x_ref[r, :], g2_ref[...]).astype(BF16)
        return (jnp.dot(h, wg_ref[...], preferred_element_type=F32),
                jnp.dot(h, wu_ref[...], preferred_element_type=F32))

    def finish(r, gate, up):
        act = ((gate * _sigmoid(gate)) * up).astype(BF16)
        y = x_ref[r, :] + jnp.dot(act, wd_ref[...], preferred_element_type=F32)
        o_ref[r, :] = _rms_norm(y, gf_ref[...]) if final_norm else y

    pending = {}
    for i in range(len(groups) + 1):
        if i < len(groups):
            pending[i] = gate_up(groups[i])
        if i >= 1:
            finish(groups[i - 1], *pending.pop(i - 1))


def _ffn(x2d, ln2_g, w_gate, w_up, w_down, lnf_g, final_norm):
    t = x2d.shape[0]
    tm = FFN_TILE
    assert t % tm == 0
    row = lambda i: (i, 0)
    tile = pl.BlockSpec((tm, D_MODEL), row)
    return pl.pallas_call(
        functools.partial(_ffn_kernel, final_norm=final_norm),
        grid=(t // tm,),
        in_specs=[tile, _resident((1, D_MODEL)),
                  _resident((D_MODEL, D_FF)), _resident((D_MODEL, D_FF)),
                  _resident((D_FF, D_MODEL)), _resident((1, D_MODEL))],
        out_specs=tile,
        out_shape=jax.ShapeDtypeStruct((t, D_MODEL), F32),
        compiler_params=pltpu.CompilerParams(
            dimension_semantics=("parallel",), vmem_limit_bytes=VMEM_LIMIT_BYTES),
        name="ffn",
    )(x2d, ln2_g, w_gate, w_up, w_down, lnf_g)


def _rotary_tables(seq_len):
    pos = np.arange(seq_len, dtype=np.float64)

    def cos_sin(dim):
        half = dim // 2
        inv_freq = ROPE_THETA ** (-np.arange(half, dtype=np.float64) / half)
        ang = pos[:, None] * inv_freq[None, :]
        cos, sin = np.cos(ang), np.sin(ang)
        reps = LANES // dim
        return (np.tile(np.concatenate([cos, cos], axis=-1), (1, reps)).astype(np.float32),
                np.tile(np.concatenate([-sin, sin], axis=-1), (1, reps)).astype(np.float32))

    return np.concatenate(cos_sin(RET_QK_DIM) + cos_sin(ATTN_HEAD_DIM), axis=1)


def _decay_tables():
    c = RET_CHUNK
    log_gamma = np.log1p(-np.exp2(-5.0 - np.arange(RET_HEADS, dtype=np.float64)))
    idx = np.arange(c, dtype=np.float64)
    rel = idx[:, None] - idx[None, :]
    intra = np.where(rel[None] >= 0,
                     np.exp(log_gamma[:, None, None] * np.maximum(rel, 0.0)[None]), 0.0)
    q_decay = np.exp(log_gamma[:, None] * (idx + 1.0))
    k_decay = np.exp(log_gamma[:, None] * (c - 1.0 - idx))
    chunk_decay = np.exp(log_gamma * c)[:, None, None]

    def per_row(d):
        lanes = np.repeat(d.T, RET_QK_DIM, axis=1)
        return np.tile(lanes, (RET_STEP_CHUNKS, 1)).astype(np.float32)

    return (intra.astype(np.float32),
            np.broadcast_to(chunk_decay, (RET_HEADS, 1, RET_V_DIM)).astype(np.float32),
            per_row(q_decay), per_row(k_decay))


def _score_caps():
    c = ATTN_BLOCK
    qi = np.arange(c)[:, None]
    kj = np.arange(2 * c)[None, :]
    rel = c + qi - kj
    band = (rel >= 0) & (rel < WINDOW)
    later = np.where(band, np.inf, MASK_VALUE)
    first = np.where(band & (kj >= c), np.inf, MASK_VALUE)
    return np.stack([np.tile(first, (2, 1)), np.tile(later, (2, 1))]).astype(np.float32)


def kernel(x, ln1_g, w_in, b_in, ret_norm_g, w_ret_out, attn_sinks, w_attn_out, w_out,
           ln2_g, w_ffn_gate, w_ffn_up, w_ffn_down, lnf_g):
    batch, seq_len, d_model = x.shape
    depth = w_in.shape[0]
    assert d_model == D_MODEL and w_in.shape[2] == D_IN
    assert seq_len % TOKEN_TILE == 0 and seq_len % ATTN_BLOCK == 0

    tables = _rotary_tables(seq_len)
    intra, cdec, qdec_rows, kdec_rows = _decay_tables()
    caps = _score_caps()
    xs = x.reshape(batch * seq_len, d_model)
    for l in range(depth):
        packed = _inproj(xs, ln1_g[l][None], w_in[l].astype(BF16), b_in[l][None], tables,
                         (qdec_rows, kdec_rows), seq_len)
        xs = _mix_merge(xs, packed, intra, cdec, ret_norm_g[l][None], attn_sinks[l], caps,
                        w_ret_out[l].astype(BF16), w_attn_out[l].astype(BF16),
                        w_out[l].astype(BF16), batch, seq_len)
        xs = _ffn(xs, ln2_g[l][None], w_ffn_gate[l].astype(BF16), w_ffn_up[l].astype(BF16),
                  w_ffn_down[l].astype(BF16), lnf_g[None], final_norm=(l == depth - 1))
    return xs.reshape(batch, seq_len, d_model)
```

```python
import functools
import math

import jax
import jax.numpy as jnp
import numpy as np
from jax import lax
from jax.experimental import pallas as pl
from jax.experimental.pallas import tpu as pltpu

F32 = jnp.float32
BF16 = jnp.bfloat16

D_MODEL = 1024
RET_HEADS = 4
RET_QK_DIM = 128
RET_V_DIM = 256
RET_CHUNK = 128
ATTN_Q_HEADS = 16
ATTN_KV_HEADS = 2
ATTN_HEAD_DIM = 64
WINDOW = 128
ATTN_BLOCK = 128
D_FF = 2816
ROPE_THETA = 10000.0
EPS = 1e-6
MASK_VALUE = -1e30

RET_QK = RET_HEADS * RET_QK_DIM
RET_V = RET_HEADS * RET_V_DIM
ATTN_Q = ATTN_Q_HEADS * ATTN_HEAD_DIM
ATTN_KV = ATTN_KV_HEADS * ATTN_HEAD_DIM
OFF_RQ = 0
OFF_RK = OFF_RQ + RET_QK
OFF_RV = OFF_RK + RET_QK
OFF_RG = OFF_RV + RET_V
OFF_AQ = OFF_RG + RET_V
OFF_AK = OFF_AQ + ATTN_Q
OFF_AV = OFF_AK + ATTN_KV
OFF_GA = OFF_AV + ATTN_KV
OFF_GB = OFF_GA + D_MODEL
D_IN = OFF_GB + D_MODEL

PACKED = {}
for _name, _width in (("rq", RET_QK), ("rq_dec", RET_QK), ("rk", RET_QK), ("rk_dec", RET_QK),
                      ("rv", RET_V), ("swish_gate", RET_V), ("aq", ATTN_Q), ("gate_a", D_MODEL),
                      ("gate_b", D_MODEL), ("ak", ATTN_KV), ("av", ATTN_KV)):
    PACKED[_name] = (sum(w for _, w in PACKED.values()), _width)
PACKED_WIDTH = sum(w for _, w in PACKED.values())
assert PACKED["ak"][0] % (2 * ATTN_KV) == 0 and PACKED["av"][0] == PACKED["ak"][0] + ATTN_KV

LANES = 128
BF16_SUBLANES = 16
TOKEN_TILE = 512
RET_STEP_CHUNKS = TOKEN_TILE // RET_CHUNK
ATTN_STEP_BLOCKS = TOKEN_TILE // ATTN_BLOCK
MERGE_ROWS = 256
INPROJ_ROWS = 256
FFN_TILE = 1024
FFN_ROWS = 256
ATTN_QK_LEAD = 4
LOG2E = math.log2(math.e)
VMEM_LIMIT_BYTES = 56 * 1024 * 1024


def _sigmoid(x):
    return 1.0 / (1.0 + jnp.exp(-x))


def _rms_norm(x, g):
    ms = jnp.mean(x * x, axis=-1, keepdims=True)
    return (x * lax.rsqrt(ms + EPS)) * g


def _resident(shape):
    nd = len(shape)
    return pl.BlockSpec(shape, lambda *_: (0,) * nd, pipeline_mode=pl.Buffered(1))


def _rot_half128(x, cos, sin_signed):
    return x * cos + pltpu.roll(x, 64, 1) * sin_signed


def _rot_half64(x, cos, sin_signed, first_half):
    partner = jnp.where(first_half, pltpu.roll(x, 96, 1), pltpu.roll(x, 32, 1))
    return x * cos + partner * sin_signed


def _packed_views(p_ref):
    return [p_ref.at[:, off:off + width] for off, width in PACKED.values()]


def _inproj_kernel(x_ref, g_ref, w_ref, b_ref, rot_ref, qdec_ref, kdec_ref, *refs):
    n_cast = (len(refs) - 1) // 2
    p_ref = refs[n_cast]
    for src_ref, dst_ref in zip(refs[:n_cast], refs[n_cast + 1:]):
        dst_ref[...] = src_ref[...].astype(BF16)

    (rq_ref, rqd_ref, rk_ref, rkd_ref, rv_ref, sg_ref, aq_ref, ga_ref, gb_ref,
     ak_ref, av_ref) = _packed_views(p_ref)
    cr_ref, sr_ref, ca_ref, sa_ref = [rot_ref.at[:, i * LANES:(i + 1) * LANES] for i in range(4)]
    lane = lax.broadcasted_iota(jnp.int32, (INPROJ_ROWS, LANES), 1)
    first_half = (lane % ATTN_HEAD_DIM) < (ATTN_HEAD_DIM // 2)
    q_scale = RET_QK_DIM ** -0.5
    a_scale = LOG2E * ATTN_HEAD_DIM ** -0.5

    def emit(r, h):
        def proj(off, width):
            acc = jnp.dot(h, w_ref[:, off:off + width], preferred_element_type=F32)
            return acc + b_ref[:, off:off + width]

        cr, sr = cr_ref[r, :], sr_ref[r, :]
        ca, sa = ca_ref[r, :], sa_ref[r, :]
        rq = proj(OFF_RQ, RET_QK)
        rk = proj(OFF_RK, RET_QK)
        for hd in range(RET_HEADS):
            sl = slice(hd * LANES, (hd + 1) * LANES)
            q = _rot_half128(rq[:, sl], cr, sr) * q_scale
            rq_ref[r, sl] = q.astype(BF16)
            rqd_ref[r, sl] = (q * qdec_ref[r, sl]).astype(BF16)
            k = _rot_half128(rk[:, sl], cr, sr)
            rk_ref[r, sl] = k.astype(BF16)
            rkd_ref[r, sl] = (k * kdec_ref[r, sl]).astype(BF16)
        rv_ref[r, :] = proj(OFF_RV, RET_V).astype(BF16)
        rg = proj(OFF_RG, RET_V)
        sg_ref[r, :] = (rg * _sigmoid(rg)).astype(BF16)
        aq = proj(OFF_AQ, ATTN_Q)
        for c in range(ATTN_Q // LANES):
            sl = slice(c * LANES, (c + 1) * LANES)
            aq_ref[r, sl] = (_rot_half64(aq[:, sl], ca, sa, first_half) * a_scale).astype(BF16)
        akv = proj(OFF_AK, 2 * ATTN_KV)
        ak_ref[r, :] = _rot_half64(akv[:, :ATTN_KV], ca, sa, first_half).astype(BF16)
        av_ref[r, :] = akv[:, ATTN_KV:].astype(BF16)
        ga_ref[r, :] = proj(OFF_GA, D_MODEL).astype(BF16)
        gb_ref[r, :] = proj(OFF_GB, D_MODEL).astype(BF16)

    groups = [slice(i * INPROJ_ROWS, (i + 1) * INPROJ_ROWS)
              for i in range(TOKEN_TILE // INPROJ_ROWS)]
    hs = [_rms_norm(x_ref[r, :], g_ref[...]).astype(BF16) for r in groups]
    for r, h in zip(groups, hs):
        emit(r, h)


def _cast_block_spec(shape, steps):
    rows, cols = shape
    need = pl.cdiv(rows, steps)
    blk = next(b for b in range(BF16_SUBLANES, rows + 1, BF16_SUBLANES)
               if rows % b == 0 and b >= need)
    last = rows // blk - 1
    return pl.BlockSpec((blk, cols), lambda i: (jnp.minimum(i, last), 0))


def _inproj(x2d, ln_g, w_in, b_in, tables, row_decays, later_weights, seq_len):
    t = x2d.shape[0]
    tm = TOKEN_TILE
    steps = t // tm
    pos_tiles = seq_len // tm
    row = lambda i: (i, 0)
    pos = lambda i: (i % pos_tiles, 0)
    cast_specs = [_cast_block_spec(w.shape, steps) for w in later_weights]
    outs = pl.pallas_call(
        _inproj_kernel,
        grid=(steps,),
        in_specs=[
            pl.BlockSpec((tm, D_MODEL), row),
            _resident((1, D_MODEL)),
            _resident((D_MODEL, D_IN)),
            _resident((1, D_IN)),
            pl.BlockSpec((tm, 4 * LANES), pos),
            _resident((tm, RET_QK)),
            _resident((tm, RET_QK)),
        ] + cast_specs,
        out_specs=[pl.BlockSpec((tm, PACKED_WIDTH), row)] + cast_specs,
        out_shape=[jax.ShapeDtypeStruct((t, PACKED_WIDTH), BF16)]
        + [jax.ShapeDtypeStruct(w.shape, BF16) for w in later_weights],
        compiler_params=pltpu.CompilerParams(
            dimension_semantics=("arbitrary",), vmem_limit_bytes=VMEM_LIMIT_BYTES),
        name="inproj",
    )(x2d, ln_g, w_in, b_in, tables, *row_decays, *later_weights)
    return outs[0], outs[1:]


def _retention(q_ref, qd_ref, k_ref, kd_ref, v_ref, sg_ref, intra_ref, cdec_ref, gn_ref,
               o_ref, state_ref):
    c = RET_CHUNK
    units = [(hd, ci) for hd in range(RET_HEADS) for ci in range(RET_STEP_CHUNKS)]
    qk_of = lambda hd: slice(hd * RET_QK_DIM, (hd + 1) * RET_QK_DIM)
    vs_of = lambda hd: slice(hd * RET_V_DIM, (hd + 1) * RET_V_DIM)
    rows_of = lambda ci: slice(ci * c, (ci + 1) * c)

    scores, update = {}, {}
    for hd, ci in units:
        rows, qk = rows_of(ci), qk_of(hd)
        scores[hd, ci] = lax.dot_general(q_ref[rows, qk], k_ref[rows, qk],
                                         (((1,), (1,)), ((), ())), preferred_element_type=F32)
        update[hd, ci] = lax.dot_general(kd_ref[rows, qk], v_ref[rows, vs_of(hd)],
                                         (((0,), (0,)), ((), ())), preferred_element_type=F32)

    y = {}
    for hd in range(RET_HEADS):
        state = state_ref[hd]
        for ci in range(RET_STEP_CHUNKS):
            rows = rows_of(ci)
            lhs = jnp.concatenate([(scores[hd, ci] * intra_ref[hd]).astype(BF16),
                                   qd_ref[rows, qk_of(hd)]], axis=1)
            rhs = jnp.concatenate([v_ref[rows, vs_of(hd)], state.astype(BF16)], axis=0)
            y[hd, ci] = jnp.dot(lhs, rhs, preferred_element_type=F32)
            state = state * cdec_ref[hd] + update[hd, ci]
        state_ref[hd] = state

    for hd, ci in units:
        rows, vs = rows_of(ci), vs_of(hd)
        yc = y[hd, ci] - jnp.mean(y[hd, ci], axis=-1, keepdims=True)
        var = jnp.mean(yc * yc, axis=-1, keepdims=True)
        yn = (yc * lax.rsqrt(var + EPS)) * gn_ref[:, vs]
        o_ref[rows, vs] = (sg_ref[rows, vs].astype(F32) * yn).astype(BF16)


def _attention(sink_ref, q_ref, kp_ref, kc_ref, vp_ref, vc_ref, cap_ref, o_ref):
    c = ATTN_BLOCK
    kv_rows = (ATTN_STEP_BLOCKS + 1) * c
    first = pl.program_id(1) == 0
    low_kv = lax.broadcasted_iota(jnp.int32, (kv_rows, LANES), 1) < ATTN_HEAD_DIM
    low_q = lax.broadcasted_iota(jnp.int32, (c, LANES), 1) < ATTN_HEAD_DIM
    eye = (lax.broadcasted_iota(jnp.int32, (c, c), 0) == lax.broadcasted_iota(jnp.int32, (c, c), 1))
    eye2 = jnp.concatenate([eye, eye], axis=0)

    keys = jnp.concatenate([kp_ref[...], kc_ref[...]], axis=0).astype(F32)
    vals = jnp.concatenate([vp_ref[...], vc_ref[...]], axis=0).astype(F32)
    keys_sw = pltpu.roll(keys, ATTN_HEAD_DIM, 1)
    vals_sw = pltpu.roll(vals, ATTN_HEAD_DIM, 1)
    pairs_per_kv = ATTN_Q_HEADS // ATTN_KV_HEADS // 2
    neg_inf = jnp.float32(-jnp.inf)
    kk_all = [jnp.where(low_kv, keys, keys_sw).astype(BF16),
              jnp.where(low_kv, keys_sw, keys).astype(BF16)]
    vv_all = [jnp.where(low_kv, vals, vals_sw).astype(BF16),
              jnp.where(low_kv, vals_sw, vals).astype(BF16)]

    def qk_scores(p, blk):
        g = p // pairs_per_kv
        qp = q_ref[blk * c:(blk + 1) * c, p * LANES:(p + 1) * LANES]
        zero = jnp.zeros_like(qp)
        q2 = jnp.concatenate([jnp.where(low_q, qp, zero), jnp.where(low_q, zero, qp)], axis=0)
        return lax.dot_general(q2, kk_all[g][blk * c:(blk + 2) * c], (((1,), (1,)), ((), ())),
                               preferred_element_type=F32)

    def softmax_pv(p, blk, s):
        g = p // pairs_per_kv
        fill = jnp.concatenate(
            [jnp.where(eye, sink_ref[2 * p] * LOG2E, neg_inf),
             jnp.where(eye, sink_ref[2 * p + 1] * LOG2E, neg_inf)], axis=0)
        cap = cap_ref[jnp.where(first, 0, 1)] if blk == 0 else cap_ref[1]
        sp = jnp.maximum(jnp.minimum(s[:, :c], cap[:, :c]), fill)
        sc = jnp.minimum(s[:, c:], cap[:, c:])
        m = jnp.max(jnp.maximum(sp, sc), axis=-1, keepdims=True)
        ep = jnp.exp2(sp - m)
        ec = jnp.exp2(sc - m)
        denom = jnp.sum(ep + ec, axis=-1, keepdims=True)
        e = jnp.concatenate([jnp.where(eye2, 0.0, ep), ec], axis=1).astype(BF16)
        o2 = jnp.dot(e, vv_all[g][blk * c:(blk + 2) * c], preferred_element_type=F32)
        o2 = o2 * (1.0 / denom)
        o_ref[blk * c:(blk + 1) * c, p * LANES:(p + 1) * LANES] = (
            jnp.where(low_q, o2[:c], o2[c:]).astype(BF16))

    units = [(p, blk) for p in range(ATTN_Q_HEADS // 2) for blk in range(ATTN_STEP_BLOCKS)]
    pending = {}
    for i in range(len(units) + ATTN_QK_LEAD):
        if i < len(units):
            pending[i] = qk_scores(*units[i])
        if i >= ATTN_QK_LEAD:
            j = i - ATTN_QK_LEAD
            softmax_pv(*units[j], pending.pop(j))


def _mix_merge_kernel(sink_ref, p_ref, kv_prev_ref, x_ref, intra_ref, cdec_ref, gn_ref, cap_ref,
                      wr_ref, wa_ref, wo_ref, o_ref, state_ref, a_ref, b_ref):
    (rq_ref, rqd_ref, rk_ref, rkd_ref, rv_ref, sg_ref, aq_ref, ga_ref, gb_ref,
     kc_ref, vc_ref) = _packed_views(p_ref)
    kp_ref = kv_prev_ref.at[:, :ATTN_KV]
    vp_ref = kv_prev_ref.at[:, ATTN_KV:]

    @pl.when(pl.program_id(1) == 0)
    def _():
        state_ref[...] = jnp.zeros_like(state_ref)

    _retention(rq_ref, rqd_ref, rk_ref, rkd_ref, rv_ref, sg_ref, intra_ref, cdec_ref, gn_ref,
               a_ref, state_ref)
    _attention(sink_ref, aq_ref, kp_ref, kc_ref, vp_ref, vc_ref, cap_ref, b_ref)
    _merge(x_ref, a_ref, b_ref, ga_ref, gb_ref, wr_ref, wa_ref, wo_ref, o_ref)


def _mix_merge(x2d, packed, intra, cdec, gn_g, sinks, caps, w_ret_out, w_attn_out, w_out,
               batch, seq_len):
    assert RET_STEP_CHUNKS * RET_CHUNK == ATTN_STEP_BLOCKS * ATTN_BLOCK == TOKEN_TILE
    t = x2d.shape[0]
    c = ATTN_BLOCK
    rows = TOKEN_TILE
    n = seq_len // rows
    cur = lambda b, i: (b * n + i, 0)
    kv_prev = lambda b, i: ((b * n + i) * ATTN_STEP_BLOCKS - jnp.where(i == 0, 0, 1),
                            PACKED["ak"][0] // (2 * ATTN_KV))
    return pl.pallas_call(
        _mix_merge_kernel,
        grid=(batch, n),
        in_specs=[
            pl.BlockSpec(memory_space=pltpu.SMEM),
            pl.BlockSpec((rows, PACKED_WIDTH), cur),
            pl.BlockSpec((c, 2 * ATTN_KV), kv_prev),
            pl.BlockSpec((rows, D_MODEL), cur),
            _resident((RET_HEADS, RET_CHUNK, RET_CHUNK)),
            _resident((RET_HEADS, 1, RET_V_DIM)),
            _resident((1, RET_V)),
            _resident((2, 2 * c, 2 * c)),
            _resident((RET_V, D_MODEL)), _resident((ATTN_Q, D_MODEL)),
            _resident((D_MODEL, D_MODEL)),
        ],
        out_specs=pl.BlockSpec((rows, D_MODEL), cur),
        out_shape=jax.ShapeDtypeStruct((t, D_MODEL), F32),
        scratch_shapes=[pltpu.VMEM((RET_HEADS, RET_QK_DIM, RET_V_DIM), F32),
                        pltpu.VMEM((rows, RET_V), BF16), pltpu.VMEM((rows, ATTN_Q), BF16)],
        compiler_params=pltpu.CompilerParams(
            dimension_semantics=("parallel", "arbitrary"), vmem_limit_bytes=VMEM_LIMIT_BYTES),
        name="mix_merge",
    )(sinks, packed, packed, x2d, intra, cdec, gn_g, caps, w_ret_out, w_attn_out, w_out)


def _merge(x_ref, a_ref, b_ref, ga_ref, gb_ref, wr_ref, wa_ref, wo_ref, o_ref):
    groups = [slice(i * MERGE_ROWS, (i + 1) * MERGE_ROWS) for i in range(TOKEN_TILE // MERGE_ROWS)]

    def branches(r):
        return (jnp.dot(a_ref[r, :], wr_ref[...], preferred_element_type=F32),
                jnp.dot(b_ref[r, :], wa_ref[...], preferred_element_type=F32))

    def finish(r, branch_a, branch_b):
        merged = (_sigmoid(ga_ref[r, :].astype(F32)) * branch_a
                  + _sigmoid(gb_ref[r, :].astype(F32)) * branch_b)
        o_ref[r, :] = x_ref[r, :] + jnp.dot(merged.astype(BF16), wo_ref[...],
                                            preferred_element_type=F32)

    pending = {}
    for i in range(len(groups) + 1):
        if i < len(groups):
            pending[i] = branches(groups[i])
        if i >= 1:
            finish(groups[i - 1], *pending.pop(i - 1))


def _ffn_kernel(x_ref, g2_ref, wg_ref, wu_ref, wd_ref, gf_ref, o_ref, *, final_norm):
    groups = [slice(i * FFN_ROWS, (i + 1) * FFN_ROWS) for i in range(FFN_TILE // FFN_ROWS)]

    def gate_up(r):
        h = _rms_norm(x_ref[r, :], g2_ref[...]).astype(BF16)
        return (jnp.dot(h, wg_ref[...], preferred_element_type=F32),
                jnp.dot(h, wu_ref[...], preferred_element_type=F32))

    def finish(r, gate, up):
        act = ((gate * _sigmoid(gate)) * up).astype(BF16)
        y = x_ref[r, :] + jnp.dot(act, wd_ref[...], preferred_element_type=F32)
        o_ref[r, :] = _rms_norm(y, gf_ref[...]) if final_norm else y

    pending = {}
    for i in range(len(groups) + 1):
        if i < len(groups):
            pending[i] = gate_up(groups[i])
        if i >= 1:
            finish(groups[i - 1], *pending.pop(i - 1))


def _ffn(x2d, ln2_g, w_gate, w_up, w_down, lnf_g, final_norm):
    t = x2d.shape[0]
    tm = FFN_TILE
    assert t % tm == 0
    row = lambda i: (i, 0)
    tile = pl.BlockSpec((tm, D_MODEL), row)
    return pl.pallas_call(
        functools.partial(_ffn_kernel, final_norm=final_norm),
        grid=(t // tm,),
        in_specs=[tile, _resident((1, D_MODEL)),
                  _resident((D_MODEL, D_FF)), _resident((D_MODEL, D_FF)),
                  _resident((D_FF, D_MODEL)), _resident((1, D_MODEL))],
        out_specs=tile,
        out_shape=jax.ShapeDtypeStruct((t, D_MODEL), F32),
        compiler_params=pltpu.CompilerParams(
            dimension_semantics=("parallel",), vmem_limit_bytes=VMEM_LIMIT_BYTES),
        name="ffn",
    )(x2d, ln2_g, w_gate, w_up, w_down, lnf_g)


def _rotary_tables(seq_len):
    pos = np.arange(seq_len, dtype=np.float64)

    def cos_sin(dim):
        half = dim // 2
        inv_freq = ROPE_THETA ** (-np.arange(half, dtype=np.float64) / half)
        ang = pos[:, None] * inv_freq[None, :]
        cos, sin = np.cos(ang), np.sin(ang)
        reps = LANES // dim
        return (np.tile(np.concatenate([cos, cos], axis=-1), (1, reps)).astype(np.float32),
                np.tile(np.concatenate([-sin, sin], axis=-1), (1, reps)).astype(np.float32))

    return np.concatenate(cos_sin(RET_QK_DIM) + cos_sin(ATTN_HEAD_DIM), axis=1)


def _decay_tables():
    c = RET_CHUNK
    log_gamma = np.log1p(-np.exp2(-5.0 - np.arange(RET_HEADS, dtype=np.float64)))
    idx = np.arange(c, dtype=np.float64)
    rel = idx[:, None] - idx[None, :]
    intra = np.where(rel[None] >= 0,
                     np.exp(log_gamma[:, None, None] * np.maximum(rel, 0.0)[None]), 0.0)
    q_decay = np.exp(log_gamma[:, None] * (idx + 1.0))
    k_decay = np.exp(log_gamma[:, None] * (c - 1.0 - idx))
    chunk_decay = np.exp(log_gamma * c)[:, None, None]

    def per_row(d):
        lanes = np.repeat(d.T, RET_QK_DIM, axis=1)
        return np.tile(lanes, (RET_STEP_CHUNKS, 1)).astype(np.float32)

    return (intra.astype(np.float32),
            np.broadcast_to(chunk_decay, (RET_HEADS, 1, RET_V_DIM)).astype(np.float32),
            per_row(q_decay), per_row(k_decay))


def _score_caps():
    c = ATTN_BLOCK
    qi = np.arange(c)[:, None]
    kj = np.arange(2 * c)[None, :]
    rel = c + qi - kj
    band = (rel >= 0) & (rel < WINDOW)
    later = np.where(band, np.inf, MASK_VALUE)
    first = np.where(band & (kj >= c), np.inf, MASK_VALUE)
    return np.stack([np.tile(first, (2, 1)), np.tile(later, (2, 1))]).astype(np.float32)


def kernel(x, ln1_g, w_in, b_in, ret_norm_g, w_ret_out, attn_sinks, w_attn_out, w_out,
           ln2_g, w_ffn_gate, w_ffn_up, w_ffn_down, lnf_g):
    batch, seq_len, d_model = x.shape
    depth = w_in.shape[0]
    assert d_model == D_MODEL and w_in.shape[2] == D_IN
    assert seq_len % TOKEN_TILE == 0 and seq_len % ATTN_BLOCK == 0

    tables = _rotary_tables(seq_len)
    intra, cdec, qdec_rows, kdec_rows = _decay_tables()
    caps = _score_caps()
    xs = x.reshape(batch * seq_len, d_model)
    for l in range(depth):
        packed, (wr, wa, wo, wg, wu, wd) = _inproj(
            xs, ln1_g[l][None], w_in[l].astype(BF16), b_in[l][None], tables,
            (qdec_rows, kdec_rows),
            (w_ret_out[l], w_attn_out[l], w_out[l], w_ffn_gate[l], w_ffn_up[l], w_ffn_down[l]),
            seq_len)
        xs = _mix_merge(xs, packed, intra, cdec, ret_norm_g[l][None], attn_sinks[l], caps,
                        wr, wa, wo, batch, seq_len)
        xs = _ffn(xs, ln2_g[l][None], wg, wu, wd, lnf_g[None], final_norm=(l == depth - 1))
    return xs.reshape(batch, seq_len, d_model)
```

```python
import functools
import math

import jax
import jax.numpy as jnp
import numpy as np
from jax import lax
from jax.experimental import pallas as pl
from jax.experimental.pallas import tpu as pltpu

F32 = jnp.float32
BF16 = jnp.bfloat16

D_MODEL = 1024
RET_HEADS = 4
RET_QK_DIM = 128
RET_V_DIM = 256
RET_CHUNK = 128
ATTN_Q_HEADS = 16
ATTN_KV_HEADS = 2
ATTN_HEAD_DIM = 64
WINDOW = 128
ATTN_BLOCK = 128
D_FF = 2816
ROPE_THETA = 10000.0
EPS = 1e-6
MASK_VALUE = -1e30

RET_QK = RET_HEADS * RET_QK_DIM
RET_V = RET_HEADS * RET_V_DIM
ATTN_Q = ATTN_Q_HEADS * ATTN_HEAD_DIM
ATTN_KV = ATTN_KV_HEADS * ATTN_HEAD_DIM
OFF_RQ = 0
OFF_RK = OFF_RQ + RET_QK
OFF_RV = OFF_RK + RET_QK
OFF_RG = OFF_RV + RET_V
OFF_AQ = OFF_RG + RET_V
OFF_AK = OFF_AQ + ATTN_Q
OFF_AV = OFF_AK + ATTN_KV
OFF_GA = OFF_AV + ATTN_KV
OFF_GB = OFF_GA + D_MODEL
D_IN = OFF_GB + D_MODEL

PACKED = {}
for _name, _width in (("rq", RET_QK), ("rq_dec", RET_QK), ("rk", RET_QK), ("rk_dec", RET_QK),
                      ("rv", RET_V), ("swish_gate", RET_V), ("aq", ATTN_Q), ("gate_a", D_MODEL),
                      ("gate_b", D_MODEL), ("ak", ATTN_KV), ("av", ATTN_KV)):
    PACKED[_name] = (sum(w for _, w in PACKED.values()), _width)
PACKED_WIDTH = sum(w for _, w in PACKED.values())
assert PACKED["ak"][0] % (2 * ATTN_KV) == 0 and PACKED["av"][0] == PACKED["ak"][0] + ATTN_KV

LANES = 128
BF16_SUBLANES = 16
TOKEN_TILE = 512
RET_STEP_CHUNKS = TOKEN_TILE // RET_CHUNK
ATTN_STEP_BLOCKS = TOKEN_TILE // ATTN_BLOCK
MERGE_ROWS = 256
INPROJ_ROWS = 256
FFN_TILE = 1024
FFN_ROWS = 256
ATTN_QK_LEAD = 4
LOG2E = math.log2(math.e)
VMEM_LIMIT_BYTES = 56 * 1024 * 1024
FRONT_VMEM_LIMIT_BYTES = 62 * 1024 * 1024


def _sigmoid(x):
    return 1.0 / (1.0 + jnp.exp(-x))


def _rms_norm(x, g):
    ms = jnp.mean(x * x, axis=-1, keepdims=True)
    return (x * lax.rsqrt(ms + EPS)) * g


def _resident(shape):
    nd = len(shape)
    return pl.BlockSpec(shape, lambda *_: (0,) * nd, pipeline_mode=pl.Buffered(1))


assert RET_QK_DIM == LANES and LANES % ATTN_HEAD_DIM == 0


def _rot_half128(x, cos, sin_signed):
    return x * cos + pltpu.roll(x, RET_QK_DIM // 2, 1) * sin_signed


def _rot_half64(x, cos, sin_signed, first_half):
    half = ATTN_HEAD_DIM // 2
    partner = jnp.where(first_half, pltpu.roll(x, LANES - half, 1), pltpu.roll(x, half, 1))
    return x * cos + partner * sin_signed


def _packed_views(p_ref):
    return [p_ref.at[:, off:off + width] for off, width in PACKED.values()]


def _inproj_kernel(x_ref, g_ref, w_ref, b_ref, rot_ref, qdec_ref, kdec_ref, *refs):
    n_cast = (len(refs) - 1) // 2
    p_ref = refs[n_cast]
    for src_ref, dst_ref in zip(refs[:n_cast], refs[n_cast + 1:]):
        dst_ref[...] = src_ref[...].astype(BF16)
    _inproj_body(x_ref, g_ref, w_ref, b_ref, rot_ref, qdec_ref, kdec_ref, p_ref)


def _inproj_body(x_ref, g_ref, w_ref, b_ref, rot_ref, qdec_ref, kdec_ref, p_ref):
    (rq_ref, rqd_ref, rk_ref, rkd_ref, rv_ref, sg_ref, aq_ref, ga_ref, gb_ref,
     ak_ref, av_ref) = _packed_views(p_ref)
    cr_ref, sr_ref, ca_ref, sa_ref = [rot_ref.at[:, i * LANES:(i + 1) * LANES] for i in range(4)]
    lane = lax.broadcasted_iota(jnp.int32, (INPROJ_ROWS, LANES), 1)
    first_half = (lane % ATTN_HEAD_DIM) < (ATTN_HEAD_DIM // 2)
    q_scale = RET_QK_DIM ** -0.5
    a_scale = LOG2E * ATTN_HEAD_DIM ** -0.5

    def emit(r, h):
        def proj(off, width):
            acc = jnp.dot(h, w_ref[:, off:off + width], preferred_element_type=F32)
            return acc + b_ref[:, off:off + width]

        cr, sr = cr_ref[r, :], sr_ref[r, :]
        ca, sa = ca_ref[r, :], sa_ref[r, :]
        rq = proj(OFF_RQ, RET_QK)
        rk = proj(OFF_RK, RET_QK)
        for hd in range(RET_HEADS):
            sl = slice(hd * LANES, (hd + 1) * LANES)
            q = _rot_half128(rq[:, sl], cr, sr) * q_scale
            rq_ref[r, sl] = q.astype(BF16)
            rqd_ref[r, sl] = (q * qdec_ref[r, sl]).astype(BF16)
            k = _rot_half128(rk[:, sl], cr, sr)
            rk_ref[r, sl] = k.astype(BF16)
            rkd_ref[r, sl] = (k * kdec_ref[r, sl]).astype(BF16)
        rv_ref[r, :] = proj(OFF_RV, RET_V).astype(BF16)
        rg = proj(OFF_RG, RET_V)
        sg_ref[r, :] = (rg * _sigmoid(rg)).astype(BF16)
        aq = proj(OFF_AQ, ATTN_Q)
        for c in range(ATTN_Q // LANES):
            sl = slice(c * LANES, (c + 1) * LANES)
            aq_ref[r, sl] = (_rot_half64(aq[:, sl], ca, sa, first_half) * a_scale).astype(BF16)
        akv = proj(OFF_AK, 2 * ATTN_KV)
        ak_ref[r, :] = _rot_half64(akv[:, :ATTN_KV], ca, sa, first_half).astype(BF16)
        av_ref[r, :] = akv[:, ATTN_KV:].astype(BF16)
        ga_ref[r, :] = proj(OFF_GA, D_MODEL).astype(BF16)
        gb_ref[r, :] = proj(OFF_GB, D_MODEL).astype(BF16)

    groups = [slice(i * INPROJ_ROWS, (i + 1) * INPROJ_ROWS)
              for i in range(TOKEN_TILE // INPROJ_ROWS)]
    hs = [_rms_norm(x_ref[r, :], g_ref[...]).astype(BF16) for r in groups]
    for r, h in zip(groups, hs):
        emit(r, h)


def _cast_block_spec(shape, steps):
    rows, cols = shape
    need = pl.cdiv(rows, steps)
    blk = next(b for b in range(BF16_SUBLANES, rows + 1, BF16_SUBLANES)
               if rows % b == 0 and b >= need)
    last = rows // blk - 1
    return pl.BlockSpec((blk, cols), lambda i: (jnp.minimum(i, last), 0))


def _inproj(x2d, ln_g, w_in, b_in, tables, row_decays, later_weights, seq_len):
    t = x2d.shape[0]
    tm = TOKEN_TILE
    steps = t // tm
    pos_tiles = seq_len // tm
    row = lambda i: (i, 0)
    pos = lambda i: (i % pos_tiles, 0)
    cast_specs = [_cast_block_spec(w.shape, steps) for w in later_weights]
    outs = pl.pallas_call(
        _inproj_kernel,
        grid=(steps,),
        in_specs=[
            pl.BlockSpec((tm, D_MODEL), row),
            _resident((1, D_MODEL)),
            _resident((D_MODEL, D_IN)),
            _resident((1, D_IN)),
            pl.BlockSpec((tm, 4 * LANES), pos),
            _resident((tm, RET_QK)),
            _resident((tm, RET_QK)),
        ] + cast_specs,
        out_specs=[pl.BlockSpec((tm, PACKED_WIDTH), row)] + cast_specs,
        out_shape=[jax.ShapeDtypeStruct((t, PACKED_WIDTH), BF16)]
        + [jax.ShapeDtypeStruct(w.shape, BF16) for w in later_weights],
        compiler_params=pltpu.CompilerParams(
            dimension_semantics=("arbitrary",), vmem_limit_bytes=VMEM_LIMIT_BYTES),
        name="inproj",
    )(x2d, ln_g, w_in, b_in, tables, *row_decays, *later_weights)
    return outs[0], outs[1:]


def _retention(q_ref, qd_ref, k_ref, kd_ref, v_ref, sg_ref, intra_ref, cdec_ref, gn_ref,
               o_ref, state_ref):
    c = RET_CHUNK
    units = [(hd, ci) for hd in range(RET_HEADS) for ci in range(RET_STEP_CHUNKS)]
    qk_of = lambda hd: slice(hd * RET_QK_DIM, (hd + 1) * RET_QK_DIM)
    vs_of = lambda hd: slice(hd * RET_V_DIM, (hd + 1) * RET_V_DIM)
    rows_of = lambda ci: slice(ci * c, (ci + 1) * c)

    scores, update = {}, {}
    for hd, ci in units:
        rows, qk = rows_of(ci), qk_of(hd)
        scores[hd, ci] = lax.dot_general(q_ref[rows, qk], k_ref[rows, qk],
                                         (((1,), (1,)), ((), ())), preferred_element_type=F32)
        update[hd, ci] = lax.dot_general(kd_ref[rows, qk], v_ref[rows, vs_of(hd)],
                                         (((0,), (0,)), ((), ())), preferred_element_type=F32)

    y = {}
    for hd in range(RET_HEADS):
        state = state_ref[hd]
        for ci in range(RET_STEP_CHUNKS):
            rows = rows_of(ci)
            lhs = jnp.concatenate([(scores[hd, ci] * intra_ref[hd]).astype(BF16),
                                   qd_ref[rows, qk_of(hd)]], axis=1)
            rhs = jnp.concatenate([v_ref[rows, vs_of(hd)], state.astype(BF16)], axis=0)
            y[hd, ci] = jnp.dot(lhs, rhs, preferred_element_type=F32)
            state = state * cdec_ref[hd] + update[hd, ci]
        state_ref[hd] = state

    for hd, ci in units:
        rows, vs = rows_of(ci), vs_of(hd)
        yc = y[hd, ci] - jnp.mean(y[hd, ci], axis=-1, keepdims=True)
        var = jnp.mean(yc * yc, axis=-1, keepdims=True)
        yn = (yc * lax.rsqrt(var + EPS)) * gn_ref[:, vs]
        o_ref[rows, vs] = (sg_ref[rows, vs].astype(F32) * yn).astype(BF16)


def _attention(sink_ref, q_ref, kp_ref, kc_ref, vp_ref, vc_ref, cap_ref, o_ref):
    c = ATTN_BLOCK
    kv_rows = (ATTN_STEP_BLOCKS + 1) * c
    first = pl.program_id(1) == 0
    low_kv = lax.broadcasted_iota(jnp.int32, (kv_rows, LANES), 1) < ATTN_HEAD_DIM
    low_q = lax.broadcasted_iota(jnp.int32, (c, LANES), 1) < ATTN_HEAD_DIM
    eye = (lax.broadcasted_iota(jnp.int32, (c, c), 0) == lax.broadcasted_iota(jnp.int32, (c, c), 1))
    eye2 = jnp.concatenate([eye, eye], axis=0)

    keys = jnp.concatenate([kp_ref[...], kc_ref[...]], axis=0).astype(F32)
    vals = jnp.concatenate([vp_ref[...], vc_ref[...]], axis=0).astype(F32)
    keys_sw = pltpu.roll(keys, ATTN_HEAD_DIM, 1)
    vals_sw = pltpu.roll(vals, ATTN_HEAD_DIM, 1)
    pairs_per_kv = ATTN_Q_HEADS // ATTN_KV_HEADS // 2
    neg_inf = jnp.float32(-jnp.inf)
    kk_all = [jnp.where(low_kv, keys, keys_sw).astype(BF16),
              jnp.where(low_kv, keys_sw, keys).astype(BF16)]
    vv_all = [jnp.where(low_kv, vals, vals_sw).astype(BF16),
              jnp.where(low_kv, vals_sw, vals).astype(BF16)]

    def qk_scores(p, blk):
        g = p // pairs_per_kv
        qp = q_ref[blk * c:(blk + 1) * c, p * LANES:(p + 1) * LANES]
        zero = jnp.zeros_like(qp)
        q2 = jnp.concatenate([jnp.where(low_q, qp, zero), jnp.where(low_q, zero, qp)], axis=0)
        return lax.dot_general(q2, kk_all[g][blk * c:(blk + 2) * c], (((1,), (1,)), ((), ())),
                               preferred_element_type=F32)

    def softmax_pv(p, blk, s):
        g = p // pairs_per_kv
        fill = jnp.concatenate(
            [jnp.where(eye, sink_ref[2 * p] * LOG2E, neg_inf),
             jnp.where(eye, sink_ref[2 * p + 1] * LOG2E, neg_inf)], axis=0)
        cap = cap_ref[jnp.where(first, 0, 1)] if blk == 0 else cap_ref[1]
        sp = jnp.maximum(jnp.minimum(s[:, :c], cap[:, :c]), fill)
        sc = jnp.minimum(s[:, c:], cap[:, c:])
        m = jnp.max(jnp.maximum(sp, sc), axis=-1, keepdims=True)
        ep = jnp.exp2(sp - m)
        ec = jnp.exp2(sc - m)
        denom = jnp.sum(ep + ec, axis=-1, keepdims=True)
        e = jnp.concatenate([jnp.where(eye2, 0.0, ep), ec], axis=1).astype(BF16)
        o2 = jnp.dot(e, vv_all[g][blk * c:(blk + 2) * c], preferred_element_type=F32)
        o2 = o2 * (1.0 / denom)
        o_ref[blk * c:(blk + 1) * c, p * LANES:(p + 1) * LANES] = (
            jnp.where(low_q, o2[:c], o2[c:]).astype(BF16))

    units = [(p, blk) for p in range(ATTN_Q_HEADS // 2) for blk in range(ATTN_STEP_BLOCKS)]
    pending = {}
    for i in range(len(units) + ATTN_QK_LEAD):
        if i < len(units):
            pending[i] = qk_scores(*units[i])
        if i >= ATTN_QK_LEAD:
            j = i - ATTN_QK_LEAD
            softmax_pv(*units[j], pending.pop(j))


def _mix_merge_kernel(sink_ref, p_ref, kv_prev_ref, x_ref, intra_ref, cdec_ref, gn_ref, cap_ref,
                      wr_ref, wa_ref, wo_ref, o_ref, state_ref, a_ref, b_ref):
    (rq_ref, rqd_ref, rk_ref, rkd_ref, rv_ref, sg_ref, aq_ref, ga_ref, gb_ref,
     kc_ref, vc_ref) = _packed_views(p_ref)
    kp_ref = kv_prev_ref.at[:, :ATTN_KV]
    vp_ref = kv_prev_ref.at[:, ATTN_KV:]

    @pl.when(pl.program_id(1) == 0)
    def _():
        state_ref[...] = jnp.zeros_like(state_ref)

    _retention(rq_ref, rqd_ref, rk_ref, rkd_ref, rv_ref, sg_ref, intra_ref, cdec_ref, gn_ref,
               a_ref, state_ref)
    _attention(sink_ref, aq_ref, kp_ref, kc_ref, vp_ref, vc_ref, cap_ref, b_ref)
    _merge(x_ref, a_ref, b_ref, ga_ref, gb_ref, wr_ref, wa_ref, wo_ref, o_ref)


def _mix_merge(x2d, packed, intra, cdec, gn_g, sinks, caps, w_ret_out, w_attn_out, w_out,
               batch, seq_len):
    assert RET_STEP_CHUNKS * RET_CHUNK == ATTN_STEP_BLOCKS * ATTN_BLOCK == TOKEN_TILE
    t = x2d.shape[0]
    c = ATTN_BLOCK
    rows = TOKEN_TILE
    n = seq_len // rows
    cur = lambda b, i: (b * n + i, 0)
    kv_prev = lambda b, i: ((b * n + i) * ATTN_STEP_BLOCKS - jnp.where(i == 0, 0, 1),
                            PACKED["ak"][0] // (2 * ATTN_KV))
    return pl.pallas_call(
        _mix_merge_kernel,
        grid=(batch, n),
        in_specs=[
            pl.BlockSpec(memory_space=pltpu.SMEM),
            pl.BlockSpec((rows, PACKED_WIDTH), cur),
            pl.BlockSpec((c, 2 * ATTN_KV), kv_prev),
            pl.BlockSpec((rows, D_MODEL), cur),
            _resident((RET_HEADS, RET_CHUNK, RET_CHUNK)),
            _resident((RET_HEADS, 1, RET_V_DIM)),
            _resident((1, RET_V)),
            _resident((2, 2 * c, 2 * c)),
            _resident((RET_V, D_MODEL)), _resident((ATTN_Q, D_MODEL)),
            _resident((D_MODEL, D_MODEL)),
        ],
        out_specs=pl.BlockSpec((rows, D_MODEL), cur),
        out_shape=jax.ShapeDtypeStruct((t, D_MODEL), F32),
        scratch_shapes=[pltpu.VMEM((RET_HEADS, RET_QK_DIM, RET_V_DIM), F32),
                        pltpu.VMEM((rows, RET_V), BF16), pltpu.VMEM((rows, ATTN_Q), BF16)],
        compiler_params=pltpu.CompilerParams(
            dimension_semantics=("parallel", "arbitrary"), vmem_limit_bytes=VMEM_LIMIT_BYTES),
        name="mix_merge",
    )(sinks, packed, packed, x2d, intra, cdec, gn_g, caps, w_ret_out, w_attn_out, w_out)


def _front_kernel(sink_ref, x_ref, g_ref, w_ref, bias_ref, rot_ref, qdec_ref, kdec_ref,
                  intra_ref, cdec_ref, gn_ref, cap_ref, wr_ref, wa_ref, wo_ref, *refs):
    n_cast = (len(refs) - 6) // 2
    o_ref = refs[n_cast]
    state_ref, a_ref, b_ref, p_ref, kv_prev_ref = refs[2 * n_cast + 1:]
    for src_ref, dst_ref in zip(refs[:n_cast], refs[n_cast + 1:2 * n_cast + 1]):
        dst_ref[...] = src_ref[...].astype(BF16)

    @pl.when((pl.program_id(0) == 0) & (pl.program_id(1) == 0))
    def _():
        kv_prev_ref[...] = jnp.zeros_like(kv_prev_ref)

    @pl.when(pl.program_id(1) == 0)
    def _():
        state_ref[...] = jnp.zeros_like(state_ref)

    _inproj_body(x_ref, g_ref, w_ref, bias_ref, rot_ref, qdec_ref, kdec_ref, p_ref)
    (rq_ref, rqd_ref, rk_ref, rkd_ref, rv_ref, sg_ref, aq_ref, ga_ref, gb_ref,
     kc_ref, vc_ref) = _packed_views(p_ref)
    kp_ref = kv_prev_ref.at[:, :ATTN_KV]
    vp_ref = kv_prev_ref.at[:, ATTN_KV:]
    _retention(rq_ref, rqd_ref, rk_ref, rkd_ref, rv_ref, sg_ref, intra_ref, cdec_ref, gn_ref,
               a_ref, state_ref)
    _attention(sink_ref, aq_ref, kp_ref, kc_ref, vp_ref, vc_ref, cap_ref, b_ref)
    _merge(x_ref, a_ref, b_ref, ga_ref, gb_ref, wr_ref, wa_ref, wo_ref, o_ref)
    kv_off = PACKED["ak"][0]
    kv_prev_ref[...] = p_ref[TOKEN_TILE - ATTN_BLOCK:, kv_off:kv_off + 2 * ATTN_KV]


def _front(x2d, ln_g, w_in, b_in, tables, row_decays, intra, cdec, gn_g, sinks, caps,
           w_ret_out, w_attn_out, w_out, later_weights, batch, seq_len):
    t = x2d.shape[0]
    c = ATTN_BLOCK
    rows = TOKEN_TILE
    n = seq_len // rows
    cur = lambda b, i: (b * n + i, 0)
    pos = lambda b, i: (i, 0)
    cast_specs = []
    for w in later_weights:
        spec = _cast_block_spec(w.shape, batch * n)
        flat = spec.index_map
        cast_specs.append(pl.BlockSpec(spec.block_shape, lambda b, i, flat=flat: flat(b * n + i)))
    outs = pl.pallas_call(
        _front_kernel,
        grid=(batch, n),
        in_specs=[
            pl.BlockSpec(memory_space=pltpu.SMEM),
            pl.BlockSpec((rows, D_MODEL), cur),
            _resident((1, D_MODEL)),
            _resident((D_MODEL, D_IN)),
            _resident((1, D_IN)),
            pl.BlockSpec((rows, 4 * LANES), pos),
            _resident((rows, RET_QK)),
            _resident((rows, RET_QK)),
            _resident((RET_HEADS, RET_CHUNK, RET_CHUNK)),
            _resident((RET_HEADS, 1, RET_V_DIM)),
            _resident((1, RET_V)),
            _resident((2, 2 * c, 2 * c)),
            _resident((RET_V, D_MODEL)), _resident((ATTN_Q, D_MODEL)),
            _resident((D_MODEL, D_MODEL)),
        ] + cast_specs,
        out_specs=[pl.BlockSpec((rows, D_MODEL), cur)] + cast_specs,
        out_shape=[jax.ShapeDtypeStruct((t, D_MODEL), F32)]
        + [jax.ShapeDtypeStruct(w.shape, BF16) for w in later_weights],
        scratch_shapes=[pltpu.VMEM((RET_HEADS, RET_QK_DIM, RET_V_DIM), F32),
                        pltpu.VMEM((rows, RET_V), BF16), pltpu.VMEM((rows, ATTN_Q), BF16),
                        pltpu.VMEM((rows, PACKED_WIDTH), BF16),
                        pltpu.VMEM((c, 2 * ATTN_KV), BF16)],
        compiler_params=pltpu.CompilerParams(
            dimension_semantics=("arbitrary", "arbitrary"), vmem_limit_bytes=FRONT_VMEM_LIMIT_BYTES),
        name="front",
    )(sinks, x2d, ln_g, w_in, b_in, tables, *row_decays, intra, cdec, gn_g, caps,
      w_ret_out, w_attn_out, w_out, *later_weights)
    return outs[0], outs[1:]


def _merge(x_ref, a_ref, b_ref, ga_ref, gb_ref, wr_ref, wa_ref, wo_ref, o_ref):
    groups = [slice(i * MERGE_ROWS, (i + 1) * MERGE_ROWS) for i in range(TOKEN_TILE // MERGE_ROWS)]

    def branches(r):
        return (jnp.dot(a_ref[r, :], wr_ref[...], preferred_element_type=F32),
                jnp.dot(b_ref[r, :], wa_ref[...], preferred_element_type=F32))

    def finish(r, branch_a, branch_b):
        merged = (_sigmoid(ga_ref[r, :].astype(F32)) * branch_a
                  + _sigmoid(gb_ref[r, :].astype(F32)) * branch_b)
        o_ref[r, :] = x_ref[r, :] + jnp.dot(merged.astype(BF16), wo_ref[...],
                                            preferred_element_type=F32)

    pending = {}
    for i in range(len(groups) + 1):
        if i < len(groups):
            pending[i] = branches(groups[i])
        if i >= 1:
            finish(groups[i - 1], *pending.pop(i - 1))


def _ffn_kernel(x_ref, g2_ref, wg_ref, wu_ref, wd_ref, gf_ref, o_ref, *, final_norm):
    groups = [slice(i * FFN_ROWS, (i + 1) * FFN_ROWS) for i in range(FFN_TILE // FFN_ROWS)]

    def gate_up(r):
        h = _rms_norm(x_ref[r, :], g2_ref[...]).astype(BF16)
        return (jnp.dot(h, wg_ref[...], preferred_element_type=F32),
                jnp.dot(h, wu_ref[...], preferred_element_type=F32))

    def finish(r, gate, up):
        act = ((gate * _sigmoid(gate)) * up).astype(BF16)
        y = x_ref[r, :] + jnp.dot(act, wd_ref[...], preferred_element_type=F32)
        o_ref[r, :] = _rms_norm(y, gf_ref[...]) if final_norm else y

    pending = {}
    for i in range(len(groups) + 1):
        if i < len(groups):
            pending[i] = gate_up(groups[i])
        if i >= 1:
            finish(groups[i - 1], *pending.pop(i - 1))


def _ffn(x2d, ln2_g, w_gate, w_up, w_down, lnf_g, final_norm):
    t = x2d.shape[0]
    tm = FFN_TILE
    assert t % tm == 0
    row = lambda i: (i, 0)
    tile = pl.BlockSpec((tm, D_MODEL), row)
    return pl.pallas_call(
        functools.partial(_ffn_kernel, final_norm=final_norm),
        grid=(t // tm,),
        in_specs=[tile, _resident((1, D_MODEL)),
                  _resident((D_MODEL, D_FF)), _resident((D_MODEL, D_FF)),
                  _resident((D_FF, D_MODEL)), _resident((1, D_MODEL))],
        out_specs=tile,
        out_shape=jax.ShapeDtypeStruct((t, D_MODEL), F32),
        compiler_params=pltpu.CompilerParams(
            dimension_semantics=("parallel",), vmem_limit_bytes=VMEM_LIMIT_BYTES),
        name="ffn",
    )(x2d, ln2_g, w_gate, w_up, w_down, lnf_g)


def _rotary_tables(seq_len):
    pos = np.arange(seq_len, dtype=np.float64)

    def cos_sin(dim):
        half = dim // 2
        inv_freq = ROPE_THETA ** (-np.arange(half, dtype=np.float64) / half)
        ang = pos[:, None] * inv_freq[None, :]
        cos, sin = np.cos(ang), np.sin(ang)
        reps = LANES // dim
        return (np.tile(np.concatenate([cos, cos], axis=-1), (1, reps)).astype(np.float32),
                np.tile(np.concatenate([-sin, sin], axis=-1), (1, reps)).astype(np.float32))

    return np.concatenate(cos_sin(RET_QK_DIM) + cos_sin(ATTN_HEAD_DIM), axis=1)


def _decay_tables():
    c = RET_CHUNK
    log_gamma = np.log1p(-np.exp2(-5.0 - np.arange(RET_HEADS, dtype=np.float64)))
    idx = np.arange(c, dtype=np.float64)
    rel = idx[:, None] - idx[None, :]
    intra = np.where(rel[None] >= 0,
                     np.exp(log_gamma[:, None, None] * np.maximum(rel, 0.0)[None]), 0.0)
    q_decay = np.exp(log_gamma[:, None] * (idx + 1.0))
    k_decay = np.exp(log_gamma[:, None] * (c - 1.0 - idx))
    chunk_decay = np.exp(log_gamma * c)[:, None, None]

    def per_row(d):
        lanes = np.repeat(d.T, RET_QK_DIM, axis=1)
        return np.tile(lanes, (RET_STEP_CHUNKS, 1)).astype(np.float32)

    return (intra.astype(np.float32),
            np.broadcast_to(chunk_decay, (RET_HEADS, 1, RET_V_DIM)).astype(np.float32),
            per_row(q_decay), per_row(k_decay))


def _score_caps():
    c = ATTN_BLOCK
    qi = np.arange(c)[:, None]
    kj = np.arange(2 * c)[None, :]
    rel = c + qi - kj
    band = (rel >= 0) & (rel < WINDOW)
    later = np.where(band, np.inf, MASK_VALUE)
    first = np.where(band & (kj >= c), np.inf, MASK_VALUE)
    return np.stack([np.tile(first, (2, 1)), np.tile(later, (2, 1))]).astype(np.float32)


def kernel(x, ln1_g, w_in, b_in, ret_norm_g, w_ret_out, attn_sinks, w_attn_out, w_out,
           ln2_g, w_ffn_gate, w_ffn_up, w_ffn_down, lnf_g):
    batch, seq_len, d_model = x.shape
    depth = w_in.shape[0]
    assert d_model == D_MODEL and w_in.shape[2] == D_IN
    assert seq_len % TOKEN_TILE == 0 and seq_len % ATTN_BLOCK == 0

    tables = _rotary_tables(seq_len)
    intra, cdec, qdec_rows, kdec_rows = _decay_tables()
    caps = _score_caps()
    xs = x.reshape(batch * seq_len, d_model)
    for l in range(depth):
        xs, (wg, wu, wd) = _front(
            xs, ln1_g[l][None], w_in[l].astype(BF16), b_in[l][None], tables,
            (qdec_rows, kdec_rows), intra, cdec, ret_norm_g[l][None], attn_sinks[l], caps,
            w_ret_out[l].astype(BF16), w_attn_out[l].astype(BF16), w_out[l].astype(BF16),
            (w_ffn_gate[l], w_ffn_up[l], w_ffn_down[l]), batch, seq_len)
        xs = _ffn(xs, ln2_g[l][None], wg, wu, wd, lnf_g[None], final_norm=(l == depth - 1))
    return xs.reshape(batch, seq_len, d_model)
```

```python
import functools
import math

import jax
import jax.numpy as jnp
import numpy as np
from jax import lax
from jax.experimental import pallas as pl
from jax.experimental.pallas import tpu as pltpu

F32 = jnp.float32
BF16 = jnp.bfloat16

D_MODEL = 1024
RET_HEADS = 4
RET_QK_DIM = 128
RET_V_DIM = 256
RET_CHUNK = 128
ATTN_Q_HEADS = 16
ATTN_KV_HEADS = 2
ATTN_HEAD_DIM = 64
WINDOW = 128
ATTN_BLOCK = 128
D_FF = 2816
ROPE_THETA = 10000.0
EPS = 1e-6
MASK_VALUE = -1e30

RET_QK = RET_HEADS * RET_QK_DIM
RET_V = RET_HEADS * RET_V_DIM
ATTN_Q = ATTN_Q_HEADS * ATTN_HEAD_DIM
ATTN_KV = ATTN_KV_HEADS * ATTN_HEAD_DIM
OFF_RQ = 0
OFF_RK = OFF_RQ + RET_QK
OFF_RV = OFF_RK + RET_QK
OFF_RG = OFF_RV + RET_V
OFF_AQ = OFF_RG + RET_V
OFF_AK = OFF_AQ + ATTN_Q
OFF_AV = OFF_AK + ATTN_KV
OFF_GA = OFF_AV + ATTN_KV
OFF_GB = OFF_GA + D_MODEL
D_IN = OFF_GB + D_MODEL

PACKED = {}
for _name, _width in (("rq", RET_QK), ("rq_dec", RET_QK), ("rk", RET_QK), ("rk_dec", RET_QK),
                      ("rv", RET_V), ("swish_gate", RET_V), ("aq", ATTN_Q), ("gate_a", D_MODEL),
                      ("gate_b", D_MODEL), ("ak", ATTN_KV), ("av", ATTN_KV)):
    PACKED[_name] = (sum(w for _, w in PACKED.values()), _width)
PACKED_WIDTH = sum(w for _, w in PACKED.values())
assert PACKED["ak"][0] % (2 * ATTN_KV) == 0 and PACKED["av"][0] == PACKED["ak"][0] + ATTN_KV

LANES = 128
BF16_SUBLANES = 16
TOKEN_TILE = 512
RET_STEP_CHUNKS = TOKEN_TILE // RET_CHUNK
ATTN_STEP_BLOCKS = TOKEN_TILE // ATTN_BLOCK
MERGE_ROWS = 256
INPROJ_ROWS = 256
FFN_TILE = 1024
FFN_ROWS = 256
ATTN_QK_LEAD = 4
LOG2E = math.log2(math.e)
VMEM_LIMIT_BYTES = 56 * 1024 * 1024


def _sigmoid(x):
    return 1.0 / (1.0 + jnp.exp(-x))


def _rms_norm(x, g):
    ms = jnp.mean(x * x, axis=-1, keepdims=True)
    return (x * lax.rsqrt(ms + EPS)) * g


def _resident(shape):
    nd = len(shape)
    return pl.BlockSpec(shape, lambda *_: (0,) * nd, pipeline_mode=pl.Buffered(1))


assert RET_QK_DIM == LANES and LANES % ATTN_HEAD_DIM == 0


def _rot_half128(x, cos, sin_signed):
    return x * cos + pltpu.roll(x, RET_QK_DIM // 2, 1) * sin_signed


def _rot_half64(x, cos, sin_signed, first_half):
    half = ATTN_HEAD_DIM // 2
    partner = jnp.where(first_half, pltpu.roll(x, LANES - half, 1), pltpu.roll(x, half, 1))
    return x * cos + partner * sin_signed


def _packed_views(p_ref):
    return [p_ref.at[:, off:off + width] for off, width in PACKED.values()]


def _inproj_kernel(x_ref, g_ref, w_ref, b_ref, rot_ref, qdec_ref, kdec_ref, *refs):
    n_cast = (len(refs) - 1) // 2
    p_ref = refs[n_cast]
    for src_ref, dst_ref in zip(refs[:n_cast], refs[n_cast + 1:]):
        dst_ref[...] = src_ref[...].astype(BF16)

    (rq_ref, rqd_ref, rk_ref, rkd_ref, rv_ref, sg_ref, aq_ref, ga_ref, gb_ref,
     ak_ref, av_ref) = _packed_views(p_ref)
    cr_ref, sr_ref, ca_ref, sa_ref = [rot_ref.at[:, i * LANES:(i + 1) * LANES] for i in range(4)]
    lane = lax.broadcasted_iota(jnp.int32, (INPROJ_ROWS, LANES), 1)
    first_half = (lane % ATTN_HEAD_DIM) < (ATTN_HEAD_DIM // 2)
    q_scale = RET_QK_DIM ** -0.5
    a_scale = LOG2E * ATTN_HEAD_DIM ** -0.5

    def emit(r, h):
        def proj(off, width):
            acc = jnp.dot(h, w_ref[:, off:off + width], preferred_element_type=F32)
            return acc + b_ref[:, off:off + width]

        cr, sr = cr_ref[r, :], sr_ref[r, :]
        ca, sa = ca_ref[r, :], sa_ref[r, :]
        rq = proj(OFF_RQ, RET_QK)
        rk = proj(OFF_RK, RET_QK)
        for hd in range(RET_HEADS):
            sl = slice(hd * LANES, (hd + 1) * LANES)
            q = _rot_half128(rq[:, sl], cr, sr) * q_scale
            rq_ref[r, sl] = q.astype(BF16)
            rqd_ref[r, sl] = (q * qdec_ref[r, sl]).astype(BF16)
            k = _rot_half128(rk[:, sl], cr, sr)
            rk_ref[r, sl] = k.astype(BF16)
            rkd_ref[r, sl] = (k * kdec_ref[r, sl]).astype(BF16)
        rv_ref[r, :] = proj(OFF_RV, RET_V).astype(BF16)
        rg = proj(OFF_RG, RET_V)
        sg_ref[r, :] = (rg * _sigmoid(rg)).astype(BF16)
        aq = proj(OFF_AQ, ATTN_Q)
        for c in range(ATTN_Q // LANES):
            sl = slice(c * LANES, (c + 1) * LANES)
            aq_ref[r, sl] = (_rot_half64(aq[:, sl], ca, sa, first_half) * a_scale).astype(BF16)
        akv = proj(OFF_AK, 2 * ATTN_KV)
        ak_ref[r, :] = _rot_half64(akv[:, :ATTN_KV], ca, sa, first_half).astype(BF16)
        av_ref[r, :] = akv[:, ATTN_KV:].astype(BF16)
        ga_ref[r, :] = proj(OFF_GA, D_MODEL).astype(BF16)
        gb_ref[r, :] = proj(OFF_GB, D_MODEL).astype(BF16)

    groups = [slice(i * INPROJ_ROWS, (i + 1) * INPROJ_ROWS)
              for i in range(TOKEN_TILE // INPROJ_ROWS)]
    hs = [_rms_norm(x_ref[r, :], g_ref[...]).astype(BF16) for r in groups]
    for r, h in zip(groups, hs):
        emit(r, h)


def _cast_block_spec(shape, steps):
    rows, cols = shape
    need = pl.cdiv(rows, steps)
    blk = next(b for b in range(BF16_SUBLANES, rows + 1, BF16_SUBLANES)
               if rows % b == 0 and b >= need)
    last = rows // blk - 1
    return pl.BlockSpec((blk, cols), lambda i: (jnp.minimum(i, last), 0))


def _inproj(x2d, ln_g, w_in, b_in, tables, row_decays, later_weights, seq_len):
    t = x2d.shape[0]
    tm = TOKEN_TILE
    steps = t // tm
    pos_tiles = seq_len // tm
    row = lambda i: (i, 0)
    pos = lambda i: (i % pos_tiles, 0)
    cast_specs = [_cast_block_spec(w.shape, steps) for w in later_weights]
    outs = pl.pallas_call(
        _inproj_kernel,
        grid=(steps,),
        in_specs=[
            pl.BlockSpec((tm, D_MODEL), row),
            _resident((1, D_MODEL)),
            _resident((D_MODEL, D_IN)),
            _resident((1, D_IN)),
            pl.BlockSpec((tm, 4 * LANES), pos),
            _resident((tm, RET_QK)),
            _resident((tm, RET_QK)),
        ] + cast_specs,
        out_specs=[pl.BlockSpec((tm, PACKED_WIDTH), row)] + cast_specs,
        out_shape=[jax.ShapeDtypeStruct((t, PACKED_WIDTH), BF16)]
        + [jax.ShapeDtypeStruct(w.shape, BF16) for w in later_weights],
        compiler_params=pltpu.CompilerParams(
            dimension_semantics=("arbitrary",), vmem_limit_bytes=VMEM_LIMIT_BYTES),
        name="inproj",
    )(x2d, ln_g, w_in, b_in, tables, *row_decays, *later_weights)
    return outs[0], outs[1:]


def _retention(q_ref, qd_ref, k_ref, kd_ref, v_ref, sg_ref, intra_ref, cdec_ref, gn_ref,
               o_ref, state_ref):
    c = RET_CHUNK
    units = [(hd, ci) for hd in range(RET_HEADS) for ci in range(RET_STEP_CHUNKS)]
    qk_of = lambda hd: slice(hd * RET_QK_DIM, (hd + 1) * RET_QK_DIM)
    vs_of = lambda hd: slice(hd * RET_V_DIM, (hd + 1) * RET_V_DIM)
    rows_of = lambda ci: slice(ci * c, (ci + 1) * c)

    scores, update = {}, {}
    for hd, ci in units:
        rows, qk = rows_of(ci), qk_of(hd)
        scores[hd, ci] = lax.dot_general(q_ref[rows, qk], k_ref[rows, qk],
                                         (((1,), (1,)), ((), ())), preferred_element_type=F32)
        update[hd, ci] = lax.dot_general(kd_ref[rows, qk], v_ref[rows, vs_of(hd)],
                                         (((0,), (0,)), ((), ())), preferred_element_type=F32)

    y = {}
    for hd in range(RET_HEADS):
        state = state_ref[hd]
        for ci in range(RET_STEP_CHUNKS):
            rows = rows_of(ci)
            lhs = jnp.concatenate([(scores[hd, ci] * intra_ref[hd]).astype(BF16),
                                   qd_ref[rows, qk_of(hd)]], axis=1)
            rhs = jnp.concatenate([v_ref[rows, vs_of(hd)], state.astype(BF16)], axis=0)
            y[hd, ci] = jnp.dot(lhs, rhs, preferred_element_type=F32)
            state = state * cdec_ref[hd] + update[hd, ci]
        state_ref[hd] = state

    for hd, ci in units:
        rows, vs = rows_of(ci), vs_of(hd)
        yc = y[hd, ci] - jnp.mean(y[hd, ci], axis=-1, keepdims=True)
        var = jnp.mean(yc * yc, axis=-1, keepdims=True)
        yn = (yc * lax.rsqrt(var + EPS)) * gn_ref[:, vs]
        o_ref[rows, vs] = (sg_ref[rows, vs].astype(F32) * yn).astype(BF16)


def _attention(sink_ref, q_ref, kp_ref, kc_ref, vp_ref, vc_ref, cap_ref, o_ref):
    c = ATTN_BLOCK
    kv_rows = (ATTN_STEP_BLOCKS + 1) * c
    first = pl.program_id(1) == 0
    low_kv = lax.broadcasted_iota(jnp.int32, (kv_rows, LANES), 1) < ATTN_HEAD_DIM
    low_q = lax.broadcasted_iota(jnp.int32, (c, LANES), 1) < ATTN_HEAD_DIM
    eye = (lax.broadcasted_iota(jnp.int32, (c, c), 0) == lax.broadcasted_iota(jnp.int32, (c, c), 1))
    eye2 = jnp.concatenate([eye, eye], axis=0)

    keys = jnp.concatenate([kp_ref[...], kc_ref[...]], axis=0).astype(F32)
    vals = jnp.concatenate([vp_ref[...], vc_ref[...]], axis=0).astype(F32)
    keys_sw = pltpu.roll(keys, ATTN_HEAD_DIM, 1)
    vals_sw = pltpu.roll(vals, ATTN_HEAD_DIM, 1)
    pairs_per_kv = ATTN_Q_HEADS // ATTN_KV_HEADS // 2
    neg_inf = jnp.float32(-jnp.inf)
    kk_all = [jnp.where(low_kv, keys, keys_sw).astype(BF16),
              jnp.where(low_kv, keys_sw, keys).astype(BF16)]
    vv_all = [jnp.where(low_kv, vals, vals_sw).astype(BF16),
              jnp.where(low_kv, vals_sw, vals).astype(BF16)]

    def qk_scores(p, blk):
        g = p // pairs_per_kv
        qp = q_ref[blk * c:(blk + 1) * c, p * LANES:(p + 1) * LANES]
        zero = jnp.zeros_like(qp)
        q2 = jnp.concatenate([jnp.where(low_q, qp, zero), jnp.where(low_q, zero, qp)], axis=0)
        return lax.dot_general(q2, kk_all[g][blk * c:(blk + 2) * c], (((1,), (1,)), ((), ())),
                               preferred_element_type=F32)

    def softmax_pv(p, blk, s):
        g = p // pairs_per_kv
        fill = jnp.concatenate(
            [jnp.where(eye, sink_ref[2 * p] * LOG2E, neg_inf),
             jnp.where(eye, sink_ref[2 * p + 1] * LOG2E, neg_inf)], axis=0)
        cap = cap_ref[jnp.where(first, 0, 1)] if blk == 0 else cap_ref[1]
        sp = jnp.maximum(jnp.minimum(s[:, :c], cap[:, :c]), fill)
        sc = jnp.minimum(s[:, c:], cap[:, c:])
        m = jnp.max(jnp.maximum(sp, sc), axis=-1, keepdims=True)
        ep = jnp.exp2(sp - m)
        ec = jnp.exp2(sc - m)
        denom = jnp.sum(ep + ec, axis=-1, keepdims=True)
        e = jnp.concatenate([jnp.where(eye2, 0.0, ep), ec], axis=1).astype(BF16)
        o2 = jnp.dot(e, vv_all[g][blk * c:(blk + 2) * c], preferred_element_type=F32)
        o2 = o2 * (1.0 / denom)
        o_ref[blk * c:(blk + 1) * c, p * LANES:(p + 1) * LANES] = (
            jnp.where(low_q, o2[:c], o2[c:]).astype(BF16))

    units = [(p, blk) for p in range(ATTN_Q_HEADS // 2) for blk in range(ATTN_STEP_BLOCKS)]
    pending = {}
    for i in range(len(units) + ATTN_QK_LEAD):
        if i < len(units):
            pending[i] = qk_scores(*units[i])
        if i >= ATTN_QK_LEAD:
            j = i - ATTN_QK_LEAD
            softmax_pv(*units[j], pending.pop(j))


def _mix_merge_kernel(sink_ref, p_ref, kv_prev_ref, x_ref, intra_ref, cdec_ref, gn_ref, cap_ref,
                      wr_ref, wa_ref, wo_ref, o_ref, state_ref, a_ref, b_ref):
    (rq_ref, rqd_ref, rk_ref, rkd_ref, rv_ref, sg_ref, aq_ref, ga_ref, gb_ref,
     kc_ref, vc_ref) = _packed_views(p_ref)
    kp_ref = kv_prev_ref.at[:, :ATTN_KV]
    vp_ref = kv_prev_ref.at[:, ATTN_KV:]

    @pl.when(pl.program_id(1) == 0)
    def _():
        state_ref[...] = jnp.zeros_like(state_ref)

    _retention(rq_ref, rqd_ref, rk_ref, rkd_ref, rv_ref, sg_ref, intra_ref, cdec_ref, gn_ref,
               a_ref, state_ref)
    _attention(sink_ref, aq_ref, kp_ref, kc_ref, vp_ref, vc_ref, cap_ref, b_ref)
    _merge(x_ref, a_ref, b_ref, ga_ref, gb_ref, wr_ref, wa_ref, wo_ref, o_ref)


def _mix_merge(x2d, packed, intra, cdec, gn_g, sinks, caps, w_ret_out, w_attn_out, w_out,
               batch, seq_len):
    assert RET_STEP_CHUNKS * RET_CHUNK == ATTN_STEP_BLOCKS * ATTN_BLOCK == TOKEN_TILE
    t = x2d.shape[0]
    c = ATTN_BLOCK
    rows = TOKEN_TILE
    n = seq_len // rows
    cur = lambda b, i: (b * n + i, 0)
    kv_prev = lambda b, i: ((b * n + i) * ATTN_STEP_BLOCKS - jnp.where(i == 0, 0, 1),
                            PACKED["ak"][0] // (2 * ATTN_KV))
    return pl.pallas_call(
        _mix_merge_kernel,
        grid=(batch, n),
        in_specs=[
            pl.BlockSpec(memory_space=pltpu.SMEM),
            pl.BlockSpec((rows, PACKED_WIDTH), cur),
            pl.BlockSpec((c, 2 * ATTN_KV), kv_prev),
            pl.BlockSpec((rows, D_MODEL), cur),
            _resident((RET_HEADS, RET_CHUNK, RET_CHUNK)),
            _resident((RET_HEADS, 1, RET_V_DIM)),
            _resident((1, RET_V)),
            _resident((2, 2 * c, 2 * c)),
            _resident((RET_V, D_MODEL)), _resident((ATTN_Q, D_MODEL)),
            _resident((D_MODEL, D_MODEL)),
        ],
        out_specs=pl.BlockSpec((rows, D_MODEL), cur),
        out_shape=jax.ShapeDtypeStruct((t, D_MODEL), F32),
        scratch_shapes=[pltpu.VMEM((RET_HEADS, RET_QK_DIM, RET_V_DIM), F32),
                        pltpu.VMEM((rows, RET_V), BF16), pltpu.VMEM((rows, ATTN_Q), BF16)],
        compiler_params=pltpu.CompilerParams(
            dimension_semantics=("parallel", "arbitrary"), vmem_limit_bytes=VMEM_LIMIT_BYTES),
        name="mix_merge",
    )(sinks, packed, packed, x2d, intra, cdec, gn_g, caps, w_ret_out, w_attn_out, w_out)


def _merge(x_ref, a_ref, b_ref, ga_ref, gb_ref, wr_ref, wa_ref, wo_ref, o_ref):
    groups = [slice(i * MERGE_ROWS, (i + 1) * MERGE_ROWS) for i in range(TOKEN_TILE // MERGE_ROWS)]

    def branches(r):
        return (jnp.dot(a_ref[r, :], wr_ref[...], preferred_element_type=F32),
                jnp.dot(b_ref[r, :], wa_ref[...], preferred_element_type=F32))

    def finish(r, branch_a, branch_b):
        merged = (_sigmoid(ga_ref[r, :].astype(F32)) * branch_a
                  + _sigmoid(gb_ref[r, :].astype(F32)) * branch_b)
        o_ref[r, :] = x_ref[r, :] + jnp.dot(merged.astype(BF16), wo_ref[...],
                                            preferred_element_type=F32)

    pending = {}
    for i in range(len(groups) + 1):
        if i < len(groups):
            pending[i] = branches(groups[i])
        if i >= 1:
            finish(groups[i - 1], *pending.pop(i - 1))


def _ffn_kernel(x_ref, g2_ref, wg_ref, wu_ref, wd_ref, gf_ref, o_ref, *, final_norm):
    groups = [slice(i * FFN_ROWS, (i + 1) * FFN_ROWS) for i in range(FFN_TILE // FFN_ROWS)]

    def gate_up(r):
        h = _rms_norm(x_ref[r, :], g2_ref[...]).astype(BF16)
        return (jnp.dot(h, wg_ref[...], preferred_element_type=F32),
                jnp.dot(h, wu_ref[...], preferred_element_type=F32))

    def finish(r, gate, up):
        act = ((gate * _sigmoid(gate)) * up).astype(BF16)
        y = x_ref[r, :] + jnp.dot(act, wd_ref[...], preferred_element_type=F32)
        o_ref[r, :] = _rms_norm(y, gf_ref[...]) if final_norm else y

    pending = {}
    for i in range(len(groups) + 1):
        if i < len(groups):
            pending[i] = gate_up(groups[i])
        if i >= 1:
            finish(groups[i - 1], *pending.pop(i - 1))


def _ffn(x2d, ln2_g, w_gate, w_up, w_down, lnf_g, final_norm):
    t = x2d.shape[0]
    tm = FFN_TILE
    assert t % tm == 0
    row = lambda i: (i, 0)
    tile = pl.BlockSpec((tm, D_MODEL), row)
    return pl.pallas_call(
        functools.partial(_ffn_kernel, final_norm=final_norm),
        grid=(t // tm,),
        in_specs=[tile, _resident((1, D_MODEL)),
                  _resident((D_MODEL, D_FF)), _resident((D_MODEL, D_FF)),
                  _resident((D_FF, D_MODEL)), _resident((1, D_MODEL))],
        out_specs=tile,
        out_shape=jax.ShapeDtypeStruct((t, D_MODEL), F32),
        compiler_params=pltpu.CompilerParams(
            dimension_semantics=("parallel",), vmem_limit_bytes=VMEM_LIMIT_BYTES),
        name="ffn",
    )(x2d, ln2_g, w_gate, w_up, w_down, lnf_g)


def _rotary_tables(seq_len):
    pos = np.arange(seq_len, dtype=np.float64)

    def cos_sin(dim):
        half = dim // 2
        inv_freq = ROPE_THETA ** (-np.arange(half, dtype=np.float64) / half)
        ang = pos[:, None] * inv_freq[None, :]
        cos, sin = np.cos(ang), np.sin(ang)
        reps = LANES // dim
        return (np.tile(np.concatenate([cos, cos], axis=-1), (1, reps)).astype(np.float32),
                np.tile(np.concatenate([-sin, sin], axis=-1), (1, reps)).astype(np.float32))

    return np.concatenate(cos_sin(RET_QK_DIM) + cos_sin(ATTN_HEAD_DIM), axis=1)


def _decay_tables():
    c = RET_CHUNK
    log_gamma = np.log1p(-np.exp2(-5.0 - np.arange(RET_HEADS, dtype=np.float64)))
    idx = np.arange(c, dtype=np.float64)
    rel = idx[:, None] - idx[None, :]
    intra = np.where(rel[None] >= 0,
                     np.exp(log_gamma[:, None, None] * np.maximum(rel, 0.0)[None]), 0.0)
    q_decay = np.exp(log_gamma[:, None] * (idx + 1.0))
    k_decay = np.exp(log_gamma[:, None] * (c - 1.0 - idx))
    chunk_decay = np.exp(log_gamma * c)[:, None, None]

    def per_row(d):
        lanes = np.repeat(d.T, RET_QK_DIM, axis=1)
        return np.tile(lanes, (RET_STEP_CHUNKS, 1)).astype(np.float32)

    return (intra.astype(np.float32),
            np.broadcast_to(chunk_decay, (RET_HEADS, 1, RET_V_DIM)).astype(np.float32),
            per_row(q_decay), per_row(k_decay))


def _score_caps():
    c = ATTN_BLOCK
    qi = np.arange(c)[:, None]
    kj = np.arange(2 * c)[None, :]
    rel = c + qi - kj
    band = (rel >= 0) & (rel < WINDOW)
    later = np.where(band, np.inf, MASK_VALUE)
    first = np.where(band & (kj >= c), np.inf, MASK_VALUE)
    return np.stack([np.tile(first, (2, 1)), np.tile(later, (2, 1))]).astype(np.float32)


def kernel(x, ln1_g, w_in, b_in, ret_norm_g, w_ret_out, attn_sinks, w_attn_out, w_out,
           ln2_g, w_ffn_gate, w_ffn_up, w_ffn_down, lnf_g):
    batch, seq_len, d_model = x.shape
    depth = w_in.shape[0]
    assert d_model == D_MODEL and w_in.shape[2] == D_IN
    assert seq_len % TOKEN_TILE == 0 and seq_len % ATTN_BLOCK == 0

    tables = _rotary_tables(seq_len)
    intra, cdec, qdec_rows, kdec_rows = _decay_tables()
    caps = _score_caps()
    xs = x.reshape(batch * seq_len, d_model)
    for l in range(depth):
        packed, (wr, wa, wo, wg, wu, wd) = _inproj(
            xs, ln1_g[l][None], w_in[l].astype(BF16), b_in[l][None], tables,
            (qdec_rows, kdec_rows),
            (w_ret_out[l], w_attn_out[l], w_out[l], w_ffn_gate[l], w_ffn_up[l], w_ffn_down[l]),
            seq_len)
        xs = _mix_merge(xs, packed, intra, cdec, ret_norm_g[l][None], attn_sinks[l], caps,
                        wr, wa, wo, batch, seq_len)
        xs = _ffn(xs, ln2_g[l][None], wg, wu, wd, lnf_g[None], final_norm=(l == depth - 1))
    return xs.reshape(batch, seq_len, d_model)
```

```python
import functools
import math

import jax
import jax.numpy as jnp
import numpy as np
from jax import lax
from jax.experimental import pallas as pl
from jax.experimental.pallas import tpu as pltpu

F32 = jnp.float32
BF16 = jnp.bfloat16

D_MODEL = 1024
RET_HEADS = 4
RET_QK_DIM = 128
RET_V_DIM = 256
RET_CHUNK = 128
ATTN_Q_HEADS = 16
ATTN_KV_HEADS = 2
ATTN_HEAD_DIM = 64
WINDOW = 128
ATTN_BLOCK = 128
D_FF = 2816
ROPE_THETA = 10000.0
EPS = 1e-6
MASK_VALUE = -1e30

RET_QK = RET_HEADS * RET_QK_DIM
RET_V = RET_HEADS * RET_V_DIM
ATTN_Q = ATTN_Q_HEADS * ATTN_HEAD_DIM
ATTN_KV = ATTN_KV_HEADS * ATTN_HEAD_DIM
OFF_RQ = 0
OFF_RK = OFF_RQ + RET_QK
OFF_RV = OFF_RK + RET_QK
OFF_RG = OFF_RV + RET_V
OFF_AQ = OFF_RG + RET_V
OFF_AK = OFF_AQ + ATTN_Q
OFF_AV = OFF_AK + ATTN_KV
OFF_GA = OFF_AV + ATTN_KV
OFF_GB = OFF_GA + D_MODEL
D_IN = OFF_GB + D_MODEL

PACKED = {}
for _name, _width in (("rq", RET_QK), ("rq_dec", RET_QK), ("rk", RET_QK), ("rk_dec", RET_QK),
                      ("rv", RET_V), ("swish_gate", RET_V), ("aq", ATTN_Q), ("gate_a", D_MODEL),
                      ("gate_b", D_MODEL), ("ak", ATTN_KV), ("av", ATTN_KV)):
    PACKED[_name] = (sum(w for _, w in PACKED.values()), _width)
PACKED_WIDTH = sum(w for _, w in PACKED.values())
assert PACKED["ak"][0] % (2 * ATTN_KV) == 0 and PACKED["av"][0] == PACKED["ak"][0] + ATTN_KV

LANES = 128
BF16_SUBLANES = 16
TOKEN_TILE = 512
RET_STEP_CHUNKS = TOKEN_TILE // RET_CHUNK
ATTN_STEP_BLOCKS = TOKEN_TILE // ATTN_BLOCK
MERGE_ROWS = 256
INPROJ_ROWS = 256
FFN_TILE = 1024
FFN_ROWS = 256
ATTN_QK_LEAD = 4
LOG2E = math.log2(math.e)
VMEM_LIMIT_BYTES = 56 * 1024 * 1024


def _sigmoid(x):
    return 1.0 / (1.0 + jnp.exp(-x))


def _rms_norm(x, g):
    ms = jnp.mean(x * x, axis=-1, keepdims=True)
    return (x * lax.rsqrt(ms + EPS)) * g


def _resident(shape):
    nd = len(shape)
    return pl.BlockSpec(shape, lambda *_: (0,) * nd, pipeline_mode=pl.Buffered(1))


assert RET_QK_DIM == LANES and LANES % ATTN_HEAD_DIM == 0


def _rot_half128(x, cos, sin_signed):
    return x * cos + pltpu.roll(x, RET_QK_DIM // 2, 1) * sin_signed


def _rot_half64(x, cos, sin_signed, first_half):
    half = ATTN_HEAD_DIM // 2
    partner = jnp.where(first_half, pltpu.roll(x, LANES - half, 1), pltpu.roll(x, half, 1))
    return x * cos + partner * sin_signed


def _packed_views(p_ref):
    return [p_ref.at[:, off:off + width] for off, width in PACKED.values()]


def _inproj_kernel(x_ref, g_ref, w_ref, b_ref, rot_ref, qdec_ref, kdec_ref, *refs):
    n_cast = (len(refs) - 1) // 2
    p_ref = refs[n_cast]
    for src_ref, dst_ref in zip(refs[:n_cast], refs[n_cast + 1:]):
        dst_ref[...] = src_ref[...].astype(BF16)

    (rq_ref, rqd_ref, rk_ref, rkd_ref, rv_ref, sg_ref, aq_ref, ga_ref, gb_ref,
     ak_ref, av_ref) = _packed_views(p_ref)
    cr_ref, sr_ref, ca_ref, sa_ref = [rot_ref.at[:, i * LANES:(i + 1) * LANES] for i in range(4)]
    lane = lax.broadcasted_iota(jnp.int32, (INPROJ_ROWS, LANES), 1)
    first_half = (lane % ATTN_HEAD_DIM) < (ATTN_HEAD_DIM // 2)
    q_scale = RET_QK_DIM ** -0.5
    a_scale = LOG2E * ATTN_HEAD_DIM ** -0.5

    def emit(r, h):
        def proj(off, width):
            acc = jnp.dot(h, w_ref[:, off:off + width], preferred_element_type=F32)
            return acc + b_ref[:, off:off + width]

        cr, sr = cr_ref[r, :], sr_ref[r, :]
        ca, sa = ca_ref[r, :], sa_ref[r, :]
        rq = proj(OFF_RQ, RET_QK)
        rk = proj(OFF_RK, RET_QK)
        for hd in range(RET_HEADS):
            sl = slice(hd * LANES, (hd + 1) * LANES)
            q = _rot_half128(rq[:, sl], cr, sr) * q_scale
            rq_ref[r, sl] = q.astype(BF16)
            rqd_ref[r, sl] = (q * qdec_ref[r, sl]).astype(BF16)
            k = _rot_half128(rk[:, sl], cr, sr)
            rk_ref[r, sl] = k.astype(BF16)
            rkd_ref[r, sl] = (k * kdec_ref[r, sl]).astype(BF16)
        rv_ref[r, :] = proj(OFF_RV, RET_V).astype(BF16)
        rg = proj(OFF_RG, RET_V)
        sg_ref[r, :] = (rg * _sigmoid(rg)).astype(BF16)
        aq = proj(OFF_AQ, ATTN_Q)
        for c in range(ATTN_Q // LANES):
            sl = slice(c * LANES, (c + 1) * LANES)
            aq_ref[r, sl] = (_rot_half64(aq[:, sl], ca, sa, first_half) * a_scale).astype(BF16)
        akv = proj(OFF_AK, 2 * ATTN_KV)
        ak_ref[r, :] = _rot_half64(akv[:, :ATTN_KV], ca, sa, first_half).astype(BF16)
        av_ref[r, :] = akv[:, ATTN_KV:].astype(BF16)
        ga_ref[r, :] = proj(OFF_GA, D_MODEL).astype(BF16)
        gb_ref[r, :] = proj(OFF_GB, D_MODEL).astype(BF16)

    groups = [slice(i * INPROJ_ROWS, (i + 1) * INPROJ_ROWS)
              for i in range(TOKEN_TILE // INPROJ_ROWS)]
    hs = [_rms_norm(x_ref[r, :], g_ref[...]).astype(BF16) for r in groups]
    for r, h in zip(groups, hs):
        emit(r, h)


def _cast_block_spec(shape, steps):
    rows, cols = shape
    need = pl.cdiv(rows, steps)
    blk = next(b for b in range(BF16_SUBLANES, rows + 1, BF16_SUBLANES)
               if rows % b == 0 and b >= need)
    last = rows // blk - 1
    return pl.BlockSpec((blk, cols), lambda i: (jnp.minimum(i, last), 0))


def _inproj(x2d, ln_g, w_in, b_in, tables, row_decays, later_weights, seq_len):
    t = x2d.shape[0]
    tm = TOKEN_TILE
    steps = t // tm
    pos_tiles = seq_len // tm
    row = lambda i: (i, 0)
    pos = lambda i: (i % pos_tiles, 0)
    cast_specs = [_cast_block_spec(w.shape, steps) for w in later_weights]
    outs = pl.pallas_call(
        _inproj_kernel,
        grid=(steps,),
        in_specs=[
            pl.BlockSpec((tm, D_MODEL), row),
            _resident((1, D_MODEL)),
            _resident((D_MODEL, D_IN)),
            _resident((1, D_IN)),
            pl.BlockSpec((tm, 4 * LANES), pos),
            _resident((tm, RET_QK)),
            _resident((tm, RET_QK)),
        ] + cast_specs,
        out_specs=[pl.BlockSpec((tm, PACKED_WIDTH), row)] + cast_specs,
        out_shape=[jax.ShapeDtypeStruct((t, PACKED_WIDTH), BF16)]
        + [jax.ShapeDtypeStruct(w.shape, BF16) for w in later_weights],
        compiler_params=pltpu.CompilerParams(
            dimension_semantics=("arbitrary",), vmem_limit_bytes=VMEM_LIMIT_BYTES),
        name="inproj",
    )(x2d, ln_g, w_in, b_in, tables, *row_decays, *later_weights)
    return outs[0], outs[1:]


def _retention(q_ref, qd_ref, k_ref, kd_ref, v_ref, sg_ref, intra_ref, cdec_ref, gn_ref,
               o_ref, state_ref):
    c = RET_CHUNK
    units = [(hd, ci) for hd in range(RET_HEADS) for ci in range(RET_STEP_CHUNKS)]
    qk_of = lambda hd: slice(hd * RET_QK_DIM, (hd + 1) * RET_QK_DIM)
    vs_of = lambda hd: slice(hd * RET_V_DIM, (hd + 1) * RET_V_DIM)
    rows_of = lambda ci: slice(ci * c, (ci + 1) * c)

    scores, update = {}, {}
    for hd, ci in units:
        rows, qk = rows_of(ci), qk_of(hd)
        scores[hd, ci] = lax.dot_general(q_ref[rows, qk], k_ref[rows, qk],
                                         (((1,), (1,)), ((), ())), preferred_element_type=F32)
        update[hd, ci] = lax.dot_general(kd_ref[rows, qk], v_ref[rows, vs_of(hd)],
                                         (((0,), (0,)), ((), ())), preferred_element_type=F32)

    y = {}
    for hd in range(RET_HEADS):
        state = state_ref[hd]
        for ci in range(RET_STEP_CHUNKS):
            rows = rows_of(ci)
            lhs = jnp.concatenate([(scores[hd, ci] * intra_ref[hd]).astype(BF16),
                                   qd_ref[rows, qk_of(hd)]], axis=1)
            rhs = jnp.concatenate([v_ref[rows, vs_of(hd)], state.astype(BF16)], axis=0)
            y[hd, ci] = jnp.dot(lhs, rhs, preferred_element_type=F32)
            state = state * cdec_ref[hd] + update[hd, ci]
        state_ref[hd] = state

    for hd, ci in units:
        rows, vs = rows_of(ci), vs_of(hd)
        yc = y[hd, ci] - jnp.mean(y[hd, ci], axis=-1, keepdims=True)
        var = jnp.mean(yc * yc, axis=-1, keepdims=True)
        yn = (yc * lax.rsqrt(var + EPS)) * gn_ref[:, vs]
        o_ref[rows, vs] = (sg_ref[rows, vs].astype(F32) * yn).astype(BF16)


def _attention(sink_ref, q_ref, kp_ref, kc_ref, vp_ref, vc_ref, cap_ref, o_ref):
    c = ATTN_BLOCK
    kv_rows = (ATTN_STEP_BLOCKS + 1) * c
    first = pl.program_id(1) == 0
    low_kv = lax.broadcasted_iota(jnp.int32, (kv_rows, LANES), 1) < ATTN_HEAD_DIM
    low_q = lax.broadcasted_iota(jnp.int32, (c, LANES), 1) < ATTN_HEAD_DIM
    eye = (lax.broadcasted_iota(jnp.int32, (c, c), 0) == lax.broadcasted_iota(jnp.int32, (c, c), 1))
    eye2 = jnp.concatenate([eye, eye], axis=0)

    keys = jnp.concatenate([kp_ref[...], kc_ref[...]], axis=0).astype(F32)
    vals = jnp.concatenate([vp_ref[...], vc_ref[...]], axis=0).astype(F32)
    keys_sw = pltpu.roll(keys, ATTN_HEAD_DIM, 1)
    vals_sw = pltpu.roll(vals, ATTN_HEAD_DIM, 1)
    pairs_per_kv = ATTN_Q_HEADS // ATTN_KV_HEADS // 2
    neg_inf = jnp.float32(-jnp.inf)
    kk_all = [jnp.where(low_kv, keys, keys_sw).astype(BF16),
              jnp.where(low_kv, keys_sw, keys).astype(BF16)]
    vv_all = [jnp.where(low_kv, vals, vals_sw).astype(BF16),
              jnp.where(low_kv, vals_sw, vals).astype(BF16)]

    def qk_scores(p, blk):
        g = p // pairs_per_kv
        qp = q_ref[blk * c:(blk + 1) * c, p * LANES:(p + 1) * LANES]
        zero = jnp.zeros_like(qp)
        q2 = jnp.concatenate([jnp.where(low_q, qp, zero), jnp.where(low_q, zero, qp)], axis=0)
        return lax.dot_general(q2, kk_all[g][blk * c:(blk + 2) * c], (((1,), (1,)), ((), ())),
                               preferred_element_type=F32)

    def softmax_pv(p, blk, s):
        g = p // pairs_per_kv
        fill = jnp.concatenate(
            [jnp.where(eye, sink_ref[2 * p] * LOG2E, neg_inf),
             jnp.where(eye, sink_ref[2 * p + 1] * LOG2E, neg_inf)], axis=0)
        cap = cap_ref[jnp.where(first, 0, 1)] if blk == 0 else cap_ref[1]
        sp = jnp.maximum(jnp.minimum(s[:, :c], cap[:, :c]), fill)
        sc = jnp.minimum(s[:, c:], cap[:, c:])
        m = jnp.max(jnp.maximum(sp, sc), axis=-1, keepdims=True)
        ep = jnp.exp2(sp - m)
        ec = jnp.exp2(sc - m)
        denom = jnp.sum(ep + ec, axis=-1, keepdims=True)
        e = jnp.concatenate([jnp.where(eye2, 0.0, ep), ec], axis=1).astype(BF16)
        o2 = jnp.dot(e, vv_all[g][blk * c:(blk + 2) * c], preferred_element_type=F32)
        o2 = o2 * (1.0 / denom)
        o_ref[blk * c:(blk + 1) * c, p * LANES:(p + 1) * LANES] = (
            jnp.where(low_q, o2[:c], o2[c:]).astype(BF16))

    units = [(p, blk) for p in range(ATTN_Q_HEADS // 2) for blk in range(ATTN_STEP_BLOCKS)]
    pending = {}
    for i in range(len(units) + ATTN_QK_LEAD):
        if i < len(units):
            pending[i] = qk_scores(*units[i])
        if i >= ATTN_QK_LEAD:
            j = i - ATTN_QK_LEAD
            softmax_pv(*units[j], pending.pop(j))


def _mixers_kernel(sink_ref, p_ref, kv_prev_ref, intra_ref, cdec_ref, gn_ref, cap_ref,
                   ab_ref, state_ref):
    (rq_ref, rqd_ref, rk_ref, rkd_ref, rv_ref, sg_ref, aq_ref, _, _,
     kc_ref, vc_ref) = _packed_views(p_ref)
    kp_ref = kv_prev_ref.at[:, :ATTN_KV]
    vp_ref = kv_prev_ref.at[:, ATTN_KV:]

    @pl.when(pl.program_id(1) == 0)
    def _():
        state_ref[...] = jnp.zeros_like(state_ref)

    _retention(rq_ref, rqd_ref, rk_ref, rkd_ref, rv_ref, sg_ref, intra_ref, cdec_ref, gn_ref,
               ab_ref.at[:, :RET_V], state_ref)
    _attention(sink_ref, aq_ref, kp_ref, kc_ref, vp_ref, vc_ref, cap_ref, ab_ref.at[:, RET_V:])


def _mixers(packed, intra, cdec, gn_g, sinks, caps, batch, seq_len):
    assert RET_STEP_CHUNKS * RET_CHUNK == ATTN_STEP_BLOCKS * ATTN_BLOCK == TOKEN_TILE
    t = packed.shape[0]
    c = ATTN_BLOCK
    rows = TOKEN_TILE
    n = seq_len // rows
    cur = lambda b, i: (b * n + i, 0)
    kv_prev = lambda b, i: ((b * n + i) * ATTN_STEP_BLOCKS - jnp.where(i == 0, 0, 1),
                            PACKED["ak"][0] // (2 * ATTN_KV))
    return pl.pallas_call(
        _mixers_kernel,
        grid=(batch, n),
        in_specs=[
            pl.BlockSpec(memory_space=pltpu.SMEM),
            pl.BlockSpec((rows, PACKED_WIDTH), cur),
            pl.BlockSpec((c, 2 * ATTN_KV), kv_prev),
            _resident((RET_HEADS, RET_CHUNK, RET_CHUNK)),
            _resident((RET_HEADS, 1, RET_V_DIM)),
            _resident((1, RET_V)),
            _resident((2, 2 * c, 2 * c)),
        ],
        out_specs=pl.BlockSpec((rows, RET_V + ATTN_Q), cur),
        out_shape=jax.ShapeDtypeStruct((t, RET_V + ATTN_Q), BF16),
        scratch_shapes=[pltpu.VMEM((RET_HEADS, RET_QK_DIM, RET_V_DIM), F32)],
        compiler_params=pltpu.CompilerParams(
            dimension_semantics=("parallel", "arbitrary"), vmem_limit_bytes=VMEM_LIMIT_BYTES),
        name="mixers",
    )(sinks, packed, packed, intra, cdec, gn_g, caps)


def _attn_kernel(sink_ref, aq_ref, kv_ref, kv_prev_ref, cap_ref, b_ref):
    _attention(sink_ref, aq_ref, kv_prev_ref.at[:, :ATTN_KV], kv_ref.at[:, :ATTN_KV],
               kv_prev_ref.at[:, ATTN_KV:], kv_ref.at[:, ATTN_KV:], cap_ref, b_ref)


def _attn_call(packed, sinks, caps, batch, seq_len):
    t = packed.shape[0]
    c = ATTN_BLOCK
    rows = TOKEN_TILE
    n = seq_len // rows
    kv_col = PACKED["ak"][0] // (2 * ATTN_KV)
    assert PACKED["aq"][0] % ATTN_Q == 0
    return pl.pallas_call(
        _attn_kernel,
        grid=(batch, n),
        in_specs=[
            pl.BlockSpec(memory_space=pltpu.SMEM),
            pl.BlockSpec((rows, ATTN_Q), lambda b, i: (b * n + i, PACKED["aq"][0] // ATTN_Q)),
            pl.BlockSpec((rows, 2 * ATTN_KV), lambda b, i: (b * n + i, kv_col)),
            pl.BlockSpec((c, 2 * ATTN_KV),
                         lambda b, i: ((b * n + i) * ATTN_STEP_BLOCKS - jnp.where(i == 0, 0, 1),
                                       kv_col)),
            _resident((2, 2 * c, 2 * c)),
        ],
        out_specs=pl.BlockSpec((rows, ATTN_Q), lambda b, i: (b * n + i, 0)),
        out_shape=jax.ShapeDtypeStruct((t, ATTN_Q), BF16),
        compiler_params=pltpu.CompilerParams(
            dimension_semantics=("parallel", "arbitrary"), vmem_limit_bytes=VMEM_LIMIT_BYTES),
        name="swattn",
    )(sinks, packed, packed, packed, caps)


def _ret_merge_kernel(p_ref, b_ref, x_ref, intra_ref, cdec_ref, gn_ref, wr_ref, wa_ref, wo_ref,
                      o_ref, state_ref, a_ref):
    (rq_ref, rqd_ref, rk_ref, rkd_ref, rv_ref, sg_ref, _, ga_ref, gb_ref, _, _) = _packed_views(p_ref)

    @pl.when(pl.program_id(1) == 0)
    def _():
        state_ref[...] = jnp.zeros_like(state_ref)

    _retention(rq_ref, rqd_ref, rk_ref, rkd_ref, rv_ref, sg_ref, intra_ref, cdec_ref, gn_ref,
               a_ref, state_ref)
    _merge(x_ref, a_ref, b_ref, ga_ref, gb_ref, wr_ref, wa_ref, wo_ref, o_ref)


def _ret_merge(x2d, packed, b, intra, cdec, gn_g, w_ret_out, w_attn_out, w_out, batch, seq_len):
    t = x2d.shape[0]
    rows = TOKEN_TILE
    n = seq_len // rows
    cur = lambda bi, i: (bi * n + i, 0)
    return pl.pallas_call(
        _ret_merge_kernel,
        grid=(batch, n),
        in_specs=[
            pl.BlockSpec((rows, PACKED_WIDTH), cur),
            pl.BlockSpec((rows, ATTN_Q), cur),
            pl.BlockSpec((rows, D_MODEL), cur),
            _resident((RET_HEADS, RET_CHUNK, RET_CHUNK)),
            _resident((RET_HEADS, 1, RET_V_DIM)),
            _resident((1, RET_V)),
            _resident((RET_V, D_MODEL)), _resident((ATTN_Q, D_MODEL)),
            _resident((D_MODEL, D_MODEL)),
        ],
        out_specs=pl.BlockSpec((rows, D_MODEL), cur),
        out_shape=jax.ShapeDtypeStruct((t, D_MODEL), F32),
        scratch_shapes=[pltpu.VMEM((RET_HEADS, RET_QK_DIM, RET_V_DIM), F32),
                        pltpu.VMEM((rows, RET_V), BF16)],
        compiler_params=pltpu.CompilerParams(
            dimension_semantics=("parallel", "arbitrary"), vmem_limit_bytes=VMEM_LIMIT_BYTES),
        name="ret_merge",
    )(packed, b, x2d, intra, cdec, gn_g, w_ret_out, w_attn_out, w_out)


def _back_kernel(x_ref, ab_ref, ga_ref, gb_ref, wr_ref, wa_ref, wo_ref, g2_ref, wg_ref, wu_ref,
                 wd_ref, gf_ref, o_ref, x1_ref, *, final_norm):
    groups = [slice(i * MERGE_ROWS, (i + 1) * MERGE_ROWS) for i in range(TOKEN_TILE // MERGE_ROWS)]
    a_ref, b_ref = ab_ref.at[:, :RET_V], ab_ref.at[:, RET_V:]

    def branches(r):
        return (jnp.dot(a_ref[r, :], wr_ref[...], preferred_element_type=F32),
                jnp.dot(b_ref[r, :], wa_ref[...], preferred_element_type=F32))

    def out_proj(r, branch_a, branch_b):
        merged = (_sigmoid(ga_ref[r, :].astype(F32)) * branch_a
                  + _sigmoid(gb_ref[r, :].astype(F32)) * branch_b)
        x1_ref[r, :] = x_ref[r, :] + jnp.dot(merged.astype(BF16), wo_ref[...],
                                             preferred_element_type=F32)

    def gate_up(r):
        h = _rms_norm(x1_ref[r, :], g2_ref[...]).astype(BF16)
        return (jnp.dot(h, wg_ref[...], preferred_element_type=F32),
                jnp.dot(h, wu_ref[...], preferred_element_type=F32))

    def down(r, gate, up):
        act = ((gate * _sigmoid(gate)) * up).astype(BF16)
        y = x1_ref[r, :] + jnp.dot(act, wd_ref[...], preferred_element_type=F32)
        o_ref[r, :] = _rms_norm(y, gf_ref[...]) if final_norm else y

    br = [branches(r) for r in groups]
    for r, (branch_a, branch_b) in zip(groups, br):
        out_proj(r, branch_a, branch_b)
    gu = [gate_up(r) for r in groups]
    for r, (gate, up) in zip(groups, gu):
        down(r, gate, up)


def _back(x2d, ab, packed, w_ret_out, w_attn_out, w_out, ln2_g, w_gate, w_up, w_down, lnf_g,
          final_norm):
    t = x2d.shape[0]
    tm = TOKEN_TILE
    row = lambda i: (i, 0)
    gate_col = lambda name: (lambda i: (i, PACKED[name][0] // D_MODEL))
    assert PACKED["gate_a"][0] % D_MODEL == 0 and PACKED["gate_b"][0] % D_MODEL == 0
    return pl.pallas_call(
        functools.partial(_back_kernel, final_norm=final_norm),
        grid=(t // tm,),
        in_specs=[
            pl.BlockSpec((tm, D_MODEL), row),
            pl.BlockSpec((tm, RET_V + ATTN_Q), row),
            pl.BlockSpec((tm, D_MODEL), gate_col("gate_a")),
            pl.BlockSpec((tm, D_MODEL), gate_col("gate_b")),
            _resident((RET_V, D_MODEL)), _resident((ATTN_Q, D_MODEL)),
            _resident((D_MODEL, D_MODEL)),
            _resident((1, D_MODEL)),
            _resident((D_MODEL, D_FF)), _resident((D_MODEL, D_FF)),
            _resident((D_FF, D_MODEL)), _resident((1, D_MODEL)),
        ],
        out_specs=pl.BlockSpec((tm, D_MODEL), row),
        out_shape=jax.ShapeDtypeStruct((t, D_MODEL), F32),
        scratch_shapes=[pltpu.VMEM((tm, D_MODEL), F32)],
        compiler_params=pltpu.CompilerParams(
            dimension_semantics=("parallel",), vmem_limit_bytes=VMEM_LIMIT_BYTES),
        name="back",
    )(x2d, ab, packed, packed, w_ret_out, w_attn_out, w_out, ln2_g, w_gate, w_up, w_down, lnf_g)


def _mix_merge_kernel(sink_ref, p_ref, kv_prev_ref, x_ref, intra_ref, cdec_ref, gn_ref, cap_ref,
                      wr_ref, wa_ref, wo_ref, o_ref, state_ref, a_ref, b_ref):
    (rq_ref, rqd_ref, rk_ref, rkd_ref, rv_ref, sg_ref, aq_ref, ga_ref, gb_ref,
     kc_ref, vc_ref) = _packed_views(p_ref)
    kp_ref = kv_prev_ref.at[:, :ATTN_KV]
    vp_ref = kv_prev_ref.at[:, ATTN_KV:]

    @pl.when(pl.program_id(1) == 0)
    def _():
        state_ref[...] = jnp.zeros_like(state_ref)

    _retention(rq_ref, rqd_ref, rk_ref, rkd_ref, rv_ref, sg_ref, intra_ref, cdec_ref, gn_ref,
               a_ref, state_ref)
    _attention(sink_ref, aq_ref, kp_ref, kc_ref, vp_ref, vc_ref, cap_ref, b_ref)
    _merge(x_ref, a_ref, b_ref, ga_ref, gb_ref, wr_ref, wa_ref, wo_ref, o_ref)


def _mix_merge(x2d, packed, intra, cdec, gn_g, sinks, caps, w_ret_out, w_attn_out, w_out,
               batch, seq_len):
    assert RET_STEP_CHUNKS * RET_CHUNK == ATTN_STEP_BLOCKS * ATTN_BLOCK == TOKEN_TILE
    t = x2d.shape[0]
    c = ATTN_BLOCK
    rows = TOKEN_TILE
    n = seq_len // rows
    cur = lambda b, i: (b * n + i, 0)
    kv_prev = lambda b, i: ((b * n + i) * ATTN_STEP_BLOCKS - jnp.where(i == 0, 0, 1),
                            PACKED["ak"][0] // (2 * ATTN_KV))
    return pl.pallas_call(
        _mix_merge_kernel,
        grid=(batch, n),
        in_specs=[
            pl.BlockSpec(memory_space=pltpu.SMEM),
            pl.BlockSpec((rows, PACKED_WIDTH), cur),
            pl.BlockSpec((c, 2 * ATTN_KV), kv_prev),
            pl.BlockSpec((rows, D_MODEL), cur),
            _resident((RET_HEADS, RET_CHUNK, RET_CHUNK)),
            _resident((RET_HEADS, 1, RET_V_DIM)),
            _resident((1, RET_V)),
            _resident((2, 2 * c, 2 * c)),
            _resident((RET_V, D_MODEL)), _resident((ATTN_Q, D_MODEL)),
            _resident((D_MODEL, D_MODEL)),
        ],
        out_specs=pl.BlockSpec((rows, D_MODEL), cur),
        out_shape=jax.ShapeDtypeStruct((t, D_MODEL), F32),
        scratch_shapes=[pltpu.VMEM((RET_HEADS, RET_QK_DIM, RET_V_DIM), F32),
                        pltpu.VMEM((rows, RET_V), BF16), pltpu.VMEM((rows, ATTN_Q), BF16)],
        compiler_params=pltpu.CompilerParams(
            dimension_semantics=("parallel", "arbitrary"), vmem_limit_bytes=VMEM_LIMIT_BYTES),
        name="mix_merge",
    )(sinks, packed, packed, x2d, intra, cdec, gn_g, caps, w_ret_out, w_attn_out, w_out)


def _merge(x_ref, a_ref, b_ref, ga_ref, gb_ref, wr_ref, wa_ref, wo_ref, o_ref):
    groups = [slice(i * MERGE_ROWS, (i + 1) * MERGE_ROWS) for i in range(TOKEN_TILE // MERGE_ROWS)]

    def branches(r):
        return (jnp.dot(a_ref[r, :], wr_ref[...], preferred_element_type=F32),
                jnp.dot(b_ref[r, :], wa_ref[...], preferred_element_type=F32))

    def finish(r, branch_a, branch_b):
        merged = (_sigmoid(ga_ref[r, :].astype(F32)) * branch_a
                  + _sigmoid(gb_ref[r, :].astype(F32)) * branch_b)
        o_ref[r, :] = x_ref[r, :] + jnp.dot(merged.astype(BF16), wo_ref[...],
                                            preferred_element_type=F32)

    pending = {}
    for i in range(len(groups) + 1):
        if i < len(groups):
            pending[i] = branches(groups[i])
        if i >= 1:
            finish(groups[i - 1], *pending.pop(i - 1))


def _ffn_kernel(x_ref, g2_ref, wg_ref, wu_ref, wd_ref, gf_ref, o_ref, *, final_norm):
    groups = [slice(i * FFN_ROWS, (i + 1) * FFN_ROWS) for i in range(FFN_TILE // FFN_ROWS)]

    def gate_up(r):
        h = _rms_norm(x_ref[r, :], g2_ref[...]).astype(BF16)
        return (jnp.dot(h, wg_ref[...], preferred_element_type=F32),
                jnp.dot(h, wu_ref[...], preferred_element_type=F32))

    def finish(r, gate, up):
        act = ((gate * _sigmoid(gate)) * up).astype(BF16)
        y = x_ref[r, :] + jnp.dot(act, wd_ref[...], preferred_element_type=F32)
        o_ref[r, :] = _rms_norm(y, gf_ref[...]) if final_norm else y

    pending = {}
    for i in range(len(groups) + 1):
        if i < len(groups):
            pending[i] = gate_up(groups[i])
        if i >= 1:
            finish(groups[i - 1], *pending.pop(i - 1))


def _ffn(x2d, ln2_g, w_gate, w_up, w_down, lnf_g, final_norm):
    t = x2d.shape[0]
    tm = FFN_TILE
    assert t % tm == 0
    row = lambda i: (i, 0)
    tile = pl.BlockSpec((tm, D_MODEL), row)
    return pl.pallas_call(
        functools.partial(_ffn_kernel, final_norm=final_norm),
        grid=(t // tm,),
        in_specs=[tile, _resident((1, D_MODEL)),
                  _resident((D_MODEL, D_FF)), _resident((D_MODEL, D_FF)),
                  _resident((D_FF, D_MODEL)), _resident((1, D_MODEL))],
        out_specs=tile,
        out_shape=jax.ShapeDtypeStruct((t, D_MODEL), F32),
        compiler_params=pltpu.CompilerParams(
            dimension_semantics=("parallel",), vmem_limit_bytes=VMEM_LIMIT_BYTES),
        name="ffn",
    )(x2d, ln2_g, w_gate, w_up, w_down, lnf_g)


def _rotary_tables(seq_len):
    pos = np.arange(seq_len, dtype=np.float64)

    def cos_sin(dim):
        half = dim // 2
        inv_freq = ROPE_THETA ** (-np.arange(half, dtype=np.float64) / half)
        ang = pos[:, None] * inv_freq[None, :]
        cos, sin = np.cos(ang), np.sin(ang)
        reps = LANES // dim
        return (np.tile(np.concatenate([cos, cos], axis=-1), (1, reps)).astype(np.float32),
                np.tile(np.concatenate([-sin, sin], axis=-1), (1, reps)).astype(np.float32))

    return np.concatenate(cos_sin(RET_QK_DIM) + cos_sin(ATTN_HEAD_DIM), axis=1)


def _decay_tables():
    c = RET_CHUNK
    log_gamma = np.log1p(-np.exp2(-5.0 - np.arange(RET_HEADS, dtype=np.float64)))
    idx = np.arange(c, dtype=np.float64)
    rel = idx[:, None] - idx[None, :]
    intra = np.where(rel[None] >= 0,
                     np.exp(log_gamma[:, None, None] * np.maximum(rel, 0.0)[None]), 0.0)
    q_decay = np.exp(log_gamma[:, None] * (idx + 1.0))
    k_decay = np.exp(log_gamma[:, None] * (c - 1.0 - idx))
    chunk_decay = np.exp(log_gamma * c)[:, None, None]

    def per_row(d):
        lanes = np.repeat(d.T, RET_QK_DIM, axis=1)
        return np.tile(lanes, (RET_STEP_CHUNKS, 1)).astype(np.float32)

    return (intra.astype(np.float32),
            np.broadcast_to(chunk_decay, (RET_HEADS, 1, RET_V_DIM)).astype(np.float32),
            per_row(q_decay), per_row(k_decay))


def _score_caps():
    c = ATTN_BLOCK
    qi = np.arange(c)[:, None]
    kj = np.arange(2 * c)[None, :]
    rel = c + qi - kj
    band = (rel >= 0) & (rel < WINDOW)
    later = np.where(band, np.inf, MASK_VALUE)
    first = np.where(band & (kj >= c), np.inf, MASK_VALUE)
    return np.stack([np.tile(first, (2, 1)), np.tile(later, (2, 1))]).astype(np.float32)


def kernel(x, ln1_g, w_in, b_in, ret_norm_g, w_ret_out, attn_sinks, w_attn_out, w_out,
           ln2_g, w_ffn_gate, w_ffn_up, w_ffn_down, lnf_g):
    batch, seq_len, d_model = x.shape
    depth = w_in.shape[0]
    assert d_model == D_MODEL and w_in.shape[2] == D_IN
    assert seq_len % TOKEN_TILE == 0 and seq_len % ATTN_BLOCK == 0

    tables = _rotary_tables(seq_len)
    intra, cdec, qdec_rows, kdec_rows = _decay_tables()
    caps = _score_caps()
    xs = x.reshape(batch * seq_len, d_model)
    for l in range(depth):
        packed, (wr, wa, wo, wg, wu, wd) = _inproj(
            xs, ln1_g[l][None], w_in[l].astype(BF16), b_in[l][None], tables,
            (qdec_rows, kdec_rows),
            (w_ret_out[l], w_attn_out[l], w_out[l], w_ffn_gate[l], w_ffn_up[l], w_ffn_down[l]),
            seq_len)
        b = _attn_call(packed, attn_sinks[l], caps, batch, seq_len)
        xs = _ret_merge(xs, packed, b, intra, cdec, ret_norm_g[l][None], wr, wa, wo, batch, seq_len)
        xs = _ffn(xs, ln2_g[l][None], wg, wu, wd, lnf_g[None], final_norm=(l == depth - 1))
    return xs.reshape(batch, seq_len, d_model)
```

```python
import functools
import math

import jax
import jax.numpy as jnp
import numpy as np
from jax import lax
from jax.experimental import pallas as pl
from jax.experimental.pallas import tpu as pltpu

F32 = jnp.float32
BF16 = jnp.bfloat16

D_MODEL = 1024
RET_HEADS = 4
RET_QK_DIM = 128
RET_V_DIM = 256
RET_CHUNK = 128
ATTN_Q_HEADS = 16
ATTN_KV_HEADS = 2
ATTN_HEAD_DIM = 64
WINDOW = 128
ATTN_BLOCK = 128
D_FF = 2816
ROPE_THETA = 10000.0
EPS = 1e-6
MASK_VALUE = -1e30

RET_QK = RET_HEADS * RET_QK_DIM
RET_V = RET_HEADS * RET_V_DIM
ATTN_Q = ATTN_Q_HEADS * ATTN_HEAD_DIM
ATTN_KV = ATTN_KV_HEADS * ATTN_HEAD_DIM
OFF_RQ = 0
OFF_RK = OFF_RQ + RET_QK
OFF_RV = OFF_RK + RET_QK
OFF_RG = OFF_RV + RET_V
OFF_AQ = OFF_RG + RET_V
OFF_AK = OFF_AQ + ATTN_Q
OFF_AV = OFF_AK + ATTN_KV
OFF_GA = OFF_AV + ATTN_KV
OFF_GB = OFF_GA + D_MODEL
D_IN = OFF_GB + D_MODEL

PACKED = {}
for _name, _width in (("rq", RET_QK), ("rq_dec", RET_QK), ("rk", RET_QK), ("rk_dec", RET_QK),
                      ("rv", RET_V), ("swish_gate", RET_V), ("aq", ATTN_Q), ("gate_a", D_MODEL),
                      ("gate_b", D_MODEL), ("ak", ATTN_KV), ("av", ATTN_KV)):
    PACKED[_name] = (sum(w for _, w in PACKED.values()), _width)
PACKED_WIDTH = sum(w for _, w in PACKED.values())
assert PACKED["ak"][0] % (2 * ATTN_KV) == 0 and PACKED["av"][0] == PACKED["ak"][0] + ATTN_KV

LANES = 128
BF16_SUBLANES = 16
TOKEN_TILE = 512
RET_STEP_CHUNKS = TOKEN_TILE // RET_CHUNK
ATTN_STEP_BLOCKS = TOKEN_TILE // ATTN_BLOCK
MERGE_ROWS = 256
INPROJ_ROWS = 256
FFN_TILE = 1024
FFN_ROWS = 256
ATTN_QK_LEAD = 4
LOG2E = math.log2(math.e)
VMEM_LIMIT_BYTES = 56 * 1024 * 1024


def _sigmoid(x):
    return 1.0 / (1.0 + jnp.exp(-x))


def _rms_norm(x, g):
    ms = jnp.mean(x * x, axis=-1, keepdims=True)
    return (x * lax.rsqrt(ms + EPS)) * g


def _resident(shape):
    nd = len(shape)
    return pl.BlockSpec(shape, lambda *_: (0,) * nd, pipeline_mode=pl.Buffered(1))


assert RET_QK_DIM == LANES and LANES % ATTN_HEAD_DIM == 0


def _rot_half128(x, cos, sin_signed):
    return x * cos + pltpu.roll(x, RET_QK_DIM // 2, 1) * sin_signed


def _rot_half64(x, cos, sin_signed, first_half):
    half = ATTN_HEAD_DIM // 2
    partner = jnp.where(first_half, pltpu.roll(x, LANES - half, 1), pltpu.roll(x, half, 1))
    return x * cos + partner * sin_signed


def _packed_views(p_ref):
    return [p_ref.at[:, off:off + width] for off, width in PACKED.values()]


def _inproj_kernel(x_ref, g_ref, w_ref, b_ref, rot_ref, qdec_ref, kdec_ref, gn_ref, *refs):
    n_cast = (len(refs) - 1) // 2
    p_ref = refs[n_cast]
    for src_ref, dst_ref in zip(refs[:n_cast], refs[n_cast + 1:]):
        dst_ref[...] = src_ref[...].astype(BF16)

    (rq_ref, rqd_ref, rk_ref, rkd_ref, rv_ref, sg_ref, aq_ref, ga_ref, gb_ref,
     ak_ref, av_ref) = _packed_views(p_ref)
    cr_ref, sr_ref, ca_ref, sa_ref = [rot_ref.at[:, i * LANES:(i + 1) * LANES] for i in range(4)]
    lane = lax.broadcasted_iota(jnp.int32, (INPROJ_ROWS, LANES), 1)
    first_half = (lane % ATTN_HEAD_DIM) < (ATTN_HEAD_DIM // 2)
    q_scale = RET_QK_DIM ** -0.5
    a_scale = LOG2E * ATTN_HEAD_DIM ** -0.5

    def emit(r, h):
        def proj(off, width):
            acc = jnp.dot(h, w_ref[:, off:off + width], preferred_element_type=F32)
            return acc + b_ref[:, off:off + width]

        cr, sr = cr_ref[r, :], sr_ref[r, :]
        ca, sa = ca_ref[r, :], sa_ref[r, :]
        rq = proj(OFF_RQ, RET_QK)
        rk = proj(OFF_RK, RET_QK)
        for hd in range(RET_HEADS):
            sl = slice(hd * LANES, (hd + 1) * LANES)
            q = _rot_half128(rq[:, sl], cr, sr) * q_scale
            rq_ref[r, sl] = q.astype(BF16)
            rqd_ref[r, sl] = (q * qdec_ref[r, sl]).astype(BF16)
            k = _rot_half128(rk[:, sl], cr, sr)
            rk_ref[r, sl] = k.astype(BF16)
            rkd_ref[r, sl] = (k * kdec_ref[r, sl]).astype(BF16)
        rv_ref[r, :] = proj(OFF_RV, RET_V).astype(BF16)
        rg = proj(OFF_RG, RET_V)
        sg_ref[r, :] = ((rg * _sigmoid(rg)) * gn_ref[...]).astype(BF16)
        aq = proj(OFF_AQ, ATTN_Q)
        for c in range(ATTN_Q // LANES):
            sl = slice(c * LANES, (c + 1) * LANES)
            aq_ref[r, sl] = (_rot_half64(aq[:, sl], ca, sa, first_half) * a_scale).astype(BF16)
        akv = proj(OFF_AK, 2 * ATTN_KV)
        ak_ref[r, :] = _rot_half64(akv[:, :ATTN_KV], ca, sa, first_half).astype(BF16)
        av_ref[r, :] = akv[:, ATTN_KV:].astype(BF16)
        ga_ref[r, :] = proj(OFF_GA, D_MODEL).astype(BF16)
        gb_ref[r, :] = proj(OFF_GB, D_MODEL).astype(BF16)

    groups = [slice(i * INPROJ_ROWS, (i + 1) * INPROJ_ROWS)
              for i in range(TOKEN_TILE // INPROJ_ROWS)]
    hs = [_rms_norm(x_ref[r, :], g_ref[...]).astype(BF16) for r in groups]
    for r, h in zip(groups, hs):
        emit(r, h)


def _cast_block_spec(shape, steps):
    rows, cols = shape
    need = pl.cdiv(rows, steps)
    blk = next(b for b in range(BF16_SUBLANES, rows + 1, BF16_SUBLANES)
               if rows % b == 0 and b >= need)
    last = rows // blk - 1
    return pl.BlockSpec((blk, cols), lambda i: (jnp.minimum(i, last), 0))


def _inproj(x2d, ln_g, w_in, b_in, tables, row_decays, gn_g, later_weights, seq_len):
    t = x2d.shape[0]
    tm = TOKEN_TILE
    steps = t // tm
    pos_tiles = seq_len // tm
    row = lambda i: (i, 0)
    pos = lambda i: (i % pos_tiles, 0)
    cast_specs = [_cast_block_spec(w.shape, steps) for w in later_weights]
    outs = pl.pallas_call(
        _inproj_kernel,
        grid=(steps,),
        in_specs=[
            pl.BlockSpec((tm, D_MODEL), row),
            _resident((1, D_MODEL)),
            _resident((D_MODEL, D_IN)),
            _resident((1, D_IN)),
            pl.BlockSpec((tm, 4 * LANES), pos),
            _resident((tm, RET_QK)),
            _resident((tm, RET_QK)),
            _resident((1, RET_V)),
        ] + cast_specs,
        out_specs=[pl.BlockSpec((tm, PACKED_WIDTH), row)] + cast_specs,
        out_shape=[jax.ShapeDtypeStruct((t, PACKED_WIDTH), BF16)]
        + [jax.ShapeDtypeStruct(w.shape, BF16) for w in later_weights],
        compiler_params=pltpu.CompilerParams(
            dimension_semantics=("arbitrary",), vmem_limit_bytes=VMEM_LIMIT_BYTES),
        name="inproj",
    )(x2d, ln_g, w_in, b_in, tables, *row_decays, gn_g, *later_weights)
    return outs[0], outs[1:]


def _retention(q_ref, qd_ref, k_ref, kd_ref, v_ref, sg_ref, intra_ref, cdec_ref, o_ref, state_ref):
    c = RET_CHUNK
    units = [(hd, ci) for hd in range(RET_HEADS) for ci in range(RET_STEP_CHUNKS)]
    qk_of = lambda hd: slice(hd * RET_QK_DIM, (hd + 1) * RET_QK_DIM)
    vs_of = lambda hd: slice(hd * RET_V_DIM, (hd + 1) * RET_V_DIM)
    rows_of = lambda ci: slice(ci * c, (ci + 1) * c)

    scores, update = {}, {}
    for hd, ci in units:
        rows, qk = rows_of(ci), qk_of(hd)
        scores[hd, ci] = lax.dot_general(q_ref[rows, qk], k_ref[rows, qk],
                                         (((1,), (1,)), ((), ())), preferred_element_type=F32)
        update[hd, ci] = lax.dot_general(kd_ref[rows, qk], v_ref[rows, vs_of(hd)],
                                         (((0,), (0,)), ((), ())), preferred_element_type=F32)

    y = {}
    for hd in range(RET_HEADS):
        state = state_ref[hd]
        for ci in range(RET_STEP_CHUNKS):
            rows = rows_of(ci)
            lhs = jnp.concatenate([(scores[hd, ci] * intra_ref[hd]).astype(BF16),
                                   qd_ref[rows, qk_of(hd)]], axis=1)
            rhs = jnp.concatenate([v_ref[rows, vs_of(hd)], state.astype(BF16)], axis=0)
            y[hd, ci] = jnp.dot(lhs, rhs, preferred_element_type=F32)
            state = state * cdec_ref[hd] + update[hd, ci]
        state_ref[hd] = state

    for ci in range(RET_STEP_CHUNKS):
        for hd in range(RET_HEADS):
            rows, vs = rows_of(ci), vs_of(hd)
            yc = y[hd, ci] - jnp.mean(y[hd, ci], axis=-1, keepdims=True)
            var = jnp.mean(yc * yc, axis=-1, keepdims=True)
            yn = yc * lax.rsqrt(var + EPS)
            o_ref[rows, vs] = (sg_ref[rows, vs].astype(F32) * yn).astype(BF16)


def _attention(sink_ref, q_ref, kp_ref, kc_ref, vp_ref, vc_ref, cap_ref, o_ref):
    c = ATTN_BLOCK
    kv_rows = (ATTN_STEP_BLOCKS + 1) * c
    first = pl.program_id(1) == 0
    low_kv = lax.broadcasted_iota(jnp.int32, (kv_rows, LANES), 1) < ATTN_HEAD_DIM
    low_q = lax.broadcasted_iota(jnp.int32, (c, LANES), 1) < ATTN_HEAD_DIM
    eye = (lax.broadcasted_iota(jnp.int32, (c, c), 0) == lax.broadcasted_iota(jnp.int32, (c, c), 1))
    eye2 = jnp.concatenate([eye, eye], axis=0)

    keys = jnp.concatenate([kp_ref[...], kc_ref[...]], axis=0).astype(F32)
    vals = jnp.concatenate([vp_ref[...], vc_ref[...]], axis=0).astype(F32)
    keys_sw = pltpu.roll(keys, ATTN_HEAD_DIM, 1)
    vals_sw = pltpu.roll(vals, ATTN_HEAD_DIM, 1)
    pairs_per_kv = ATTN_Q_HEADS // ATTN_KV_HEADS // 2
    neg_inf = jnp.float32(-jnp.inf)
    kk_all = [jnp.where(low_kv, keys, keys_sw).astype(BF16),
              jnp.where(low_kv, keys_sw, keys).astype(BF16)]
    vv_all = [jnp.where(low_kv, vals, vals_sw).astype(BF16),
              jnp.where(low_kv, vals_sw, vals).astype(BF16)]

    def qk_scores(p, blk):
        g = p // pairs_per_kv
        qp = q_ref[blk * c:(blk + 1) * c, p * LANES:(p + 1) * LANES]
        zero = jnp.zeros_like(qp)
        q2 = jnp.concatenate([jnp.where(low_q, qp, zero), jnp.where(low_q, zero, qp)], axis=0)
        return lax.dot_general(q2, kk_all[g][blk * c:(blk + 2) * c], (((1,), (1,)), ((), ())),
                               preferred_element_type=F32)

    def softmax_pv(p, blk, s):
        g = p // pairs_per_kv
        fill = jnp.concatenate(
            [jnp.where(eye, sink_ref[2 * p] * LOG2E, neg_inf),
             jnp.where(eye, sink_ref[2 * p + 1] * LOG2E, neg_inf)], axis=0)
        cap = cap_ref[jnp.where(first, 0, 1)] if blk == 0 else cap_ref[1]
        sp = jnp.maximum(jnp.minimum(s[:, :c], cap[:, :c]), fill)
        sc = jnp.minimum(s[:, c:], cap[:, c:])
        m = jnp.max(jnp.maximum(sp, sc), axis=-1, keepdims=True)
        ep = jnp.exp2(sp - m)
        ec = jnp.exp2(sc - m)
        denom = jnp.sum(ep + ec, axis=-1, keepdims=True)
        e = jnp.concatenate([jnp.where(eye2, 0.0, ep), ec], axis=1).astype(BF16)
        o2 = jnp.dot(e, vv_all[g][blk * c:(blk + 2) * c], preferred_element_type=F32)
        o2 = o2 * (1.0 / denom)
        o_ref[blk * c:(blk + 1) * c, p * LANES:(p + 1) * LANES] = (
            jnp.where(low_q, o2[:c], o2[c:]).astype(BF16))

    units = [(p, blk) for p in range(ATTN_Q_HEADS // 2) for blk in range(ATTN_STEP_BLOCKS)]
    pending = {}
    for i in range(len(units) + ATTN_QK_LEAD):
        if i < len(units):
            pending[i] = qk_scores(*units[i])
        if i >= ATTN_QK_LEAD:
            j = i - ATTN_QK_LEAD
            softmax_pv(*units[j], pending.pop(j))


def _mix_merge_kernel(sink_ref, p_ref, kv_prev_ref, x_ref, intra_ref, cdec_ref, cap_ref,
                      wr_ref, wa_ref, wo_ref, o_ref, state_ref, a_ref, b_ref):
    (rq_ref, rqd_ref, rk_ref, rkd_ref, rv_ref, sg_ref, aq_ref, ga_ref, gb_ref,
     kc_ref, vc_ref) = _packed_views(p_ref)
    kp_ref = kv_prev_ref.at[:, :ATTN_KV]
    vp_ref = kv_prev_ref.at[:, ATTN_KV:]

    @pl.when(pl.program_id(1) == 0)
    def _():
        state_ref[...] = jnp.zeros_like(state_ref)

    _retention(rq_ref, rqd_ref, rk_ref, rkd_ref, rv_ref, sg_ref, intra_ref, cdec_ref,
               a_ref, state_ref)
    _attention(sink_ref, aq_ref, kp_ref, kc_ref, vp_ref, vc_ref, cap_ref, b_ref)
    _merge(x_ref, a_ref, b_ref, ga_ref, gb_ref, wr_ref, wa_ref, wo_ref, o_ref)


def _mix_merge(x2d, packed, intra, cdec, sinks, caps, w_ret_out, w_attn_out, w_out,
               batch, seq_len):
    assert RET_STEP_CHUNKS * RET_CHUNK == ATTN_STEP_BLOCKS * ATTN_BLOCK == TOKEN_TILE
    t = x2d.shape[0]
    c = ATTN_BLOCK
    rows = TOKEN_TILE
    n = seq_len // rows
    cur = lambda b, i: (b * n + i, 0)
    kv_prev = lambda b, i: ((b * n + i) * ATTN_STEP_BLOCKS - jnp.where(i == 0, 0, 1),
                            PACKED["ak"][0] // (2 * ATTN_KV))
    return pl.pallas_call(
        _mix_merge_kernel,
        grid=(batch, n),
        in_specs=[
            pl.BlockSpec(memory_space=pltpu.SMEM),
            pl.BlockSpec((rows, PACKED_WIDTH), cur),
            pl.BlockSpec((c, 2 * ATTN_KV), kv_prev),
            pl.BlockSpec((rows, D_MODEL), cur),
            _resident((RET_HEADS, RET_CHUNK, RET_CHUNK)),
            _resident((RET_HEADS, 1, RET_V_DIM)),
            _resident((2, 2 * c, 2 * c)),
            _resident((RET_V, D_MODEL)), _resident((ATTN_Q, D_MODEL)),
            _resident((D_MODEL, D_MODEL)),
        ],
        out_specs=pl.BlockSpec((rows, D_MODEL), cur),
        out_shape=jax.ShapeDtypeStruct((t, D_MODEL), F32),
        scratch_shapes=[pltpu.VMEM((RET_HEADS, RET_QK_DIM, RET_V_DIM), F32),
                        pltpu.VMEM((rows, RET_V), BF16), pltpu.VMEM((rows, ATTN_Q), BF16)],
        compiler_params=pltpu.CompilerParams(
            dimension_semantics=("parallel", "arbitrary"), vmem_limit_bytes=VMEM_LIMIT_BYTES),
        name="mix_merge",
    )(sinks, packed, packed, x2d, intra, cdec, caps, w_ret_out, w_attn_out, w_out)


def _merge(x_ref, a_ref, b_ref, ga_ref, gb_ref, wr_ref, wa_ref, wo_ref, o_ref):
    groups = [slice(i * MERGE_ROWS, (i + 1) * MERGE_ROWS) for i in range(TOKEN_TILE // MERGE_ROWS)]

    def branches(r):
        return (jnp.dot(a_ref[r, :], wr_ref[...], preferred_element_type=F32),
                jnp.dot(b_ref[r, :], wa_ref[...], preferred_element_type=F32))

    def finish(r, branch_a, branch_b):
        merged = (_sigmoid(ga_ref[r, :].astype(F32)) * branch_a
                  + _sigmoid(gb_ref[r, :].astype(F32)) * branch_b)
        o_ref[r, :] = x_ref[r, :] + jnp.dot(merged.astype(BF16), wo_ref[...],
                                            preferred_element_type=F32)

    pending = {}
    for i in range(len(groups) + 1):
        if i < len(groups):
            pending[i] = branches(groups[i])
        if i >= 1:
            finish(groups[i - 1], *pending.pop(i - 1))


def _ffn_kernel(x_ref, g2_ref, wg_ref, wu_ref, wd_ref, gf_ref, o_ref, *, final_norm):
    groups = [slice(i * FFN_ROWS, (i + 1) * FFN_ROWS) for i in range(FFN_TILE // FFN_ROWS)]

    def gate_up(r):
        h = _rms_norm(x_ref[r, :], g2_ref[...]).astype(BF16)
        return (jnp.dot(h, wg_ref[...], preferred_element_type=F32),
                jnp.dot(h, wu_ref[...], preferred_element_type=F32))

    def finish(r, gate, up):
        act = ((gate * _sigmoid(gate)) * up).astype(BF16)
        y = x_ref[r, :] + jnp.dot(act, wd_ref[...], preferred_element_type=F32)
        o_ref[r, :] = _rms_norm(y, gf_ref[...]) if final_norm else y

    pending = {}
    for i in range(len(groups) + 1):
        if i < len(groups):
            pending[i] = gate_up(groups[i])
        if i >= 1:
            finish(groups[i - 1], *pending.pop(i - 1))


def _ffn(x2d, ln2_g, w_gate, w_up, w_down, lnf_g, final_norm):
    t = x2d.shape[0]
    tm = FFN_TILE
    assert t % tm == 0
    row = lambda i: (i, 0)
    tile = pl.BlockSpec((tm, D_MODEL), row)
    return pl.pallas_call(
        functools.partial(_ffn_kernel, final_norm=final_norm),
        grid=(t // tm,),
        in_specs=[tile, _resident((1, D_MODEL)),
                  _resident((D_MODEL, D_FF)), _resident((D_MODEL, D_FF)),
                  _resident((D_FF, D_MODEL)), _resident((1, D_MODEL))],
        out_specs=tile,
        out_shape=jax.ShapeDtypeStruct((t, D_MODEL), F32),
        compiler_params=pltpu.CompilerParams(
            dimension_semantics=("parallel",), vmem_limit_bytes=VMEM_LIMIT_BYTES),
        name="ffn",
    )(x2d, ln2_g, w_gate, w_up, w_down, lnf_g)


def _rotary_tables(seq_len):
    pos = np.arange(seq_len, dtype=np.float64)

    def cos_sin(dim):
        half = dim // 2
        inv_freq = ROPE_THETA ** (-np.arange(half, dtype=np.float64) / half)
        ang = pos[:, None] * inv_freq[None, :]
        cos, sin = np.cos(ang), np.sin(ang)
        reps = LANES // dim
        return (np.tile(np.concatenate([cos, cos], axis=-1), (1, reps)).astype(np.float32),
                np.tile(np.concatenate([-sin, sin], axis=-1), (1, reps)).astype(np.float32))

    return np.concatenate(cos_sin(RET_QK_DIM) + cos_sin(ATTN_HEAD_DIM), axis=1)


def _decay_tables():
    c = RET_CHUNK
    log_gamma = np.log1p(-np.exp2(-5.0 - np.arange(RET_HEADS, dtype=np.float64)))
    idx = np.arange(c, dtype=np.float64)
    rel = idx[:, None] - idx[None, :]
    intra = np.where(rel[None] >= 0,
                     np.exp(log_gamma[:, None, None] * np.maximum(rel, 0.0)[None]), 0.0)
    q_decay = np.exp(log_gamma[:, None] * (idx + 1.0))
    k_decay = np.exp(log_gamma[:, None] * (c - 1.0 - idx))
    chunk_decay = np.exp(log_gamma * c)[:, None, None]

    def per_row(d):
        lanes = np.repeat(d.T, RET_QK_DIM, axis=1)
        return np.tile(lanes, (RET_STEP_CHUNKS, 1)).astype(np.float32)

    return (intra.astype(np.float32),
            np.broadcast_to(chunk_decay, (RET_HEADS, 1, RET_V_DIM)).astype(np.float32),
            per_row(q_decay), per_row(k_decay))


def _score_caps():
    c = ATTN_BLOCK
    qi = np.arange(c)[:, None]
    kj = np.arange(2 * c)[None, :]
    rel = c + qi - kj
    band = (rel >= 0) & (rel < WINDOW)
    later = np.where(band, np.inf, MASK_VALUE)
    first = np.where(band & (kj >= c), np.inf, MASK_VALUE)
    return np.stack([np.tile(first, (2, 1)), np.tile(later, (2, 1))]).astype(np.float32)


def kernel(x, ln1_g, w_in, b_in, ret_norm_g, w_ret_out, attn_sinks, w_attn_out, w_out,
           ln2_g, w_ffn_gate, w_ffn_up, w_ffn_down, lnf_g):
    batch, seq_len, d_model = x.shape
    depth = w_in.shape[0]
    assert d_model == D_MODEL and w_in.shape[2] == D_IN
    assert seq_len % TOKEN_TILE == 0 and seq_len % ATTN_BLOCK == 0

    tables = _rotary_tables(seq_len)
    intra, cdec, qdec_rows, kdec_rows = _decay_tables()
    caps = _score_caps()
    xs = x.reshape(batch * seq_len, d_model)
    for l in range(depth):
        packed, (wr, wa, wo, wg, wu, wd) = _inproj(
            xs, ln1_g[l][None], w_in[l].astype(BF16), b_in[l][None], tables,
            (qdec_rows, kdec_rows), ret_norm_g[l][None],
            (w_ret_out[l], w_attn_out[l], w_out[l], w_ffn_gate[l], w_ffn_up[l], w_ffn_down[l]),
            seq_len)
        xs = _mix_merge(xs, packed, intra, cdec, attn_sinks[l], caps,
                        wr, wa, wo, batch, seq_len)
        xs = _ffn(xs, ln2_g[l][None], wg, wu, wd, lnf_g[None], final_norm=(l == depth - 1))
    return xs.reshape(batch, seq_len, d_model)
```
